```python
import jax, jax.numpy as jnp
from jax import lax
import numpy as np

D_MODEL = 1024
BATCH = 32
SEQ = 256
DEPTH = 1
DEC_BATCH = 4
DEC_SEQ = 2048
PAST_LEN = 256

GRID_W = 64
GLA_HEADS = 4
GLA_DK = D_MODEL // 16
GLA_DV = D_MODEL // 8
GLA_LOWRANK = 16
GLA_GATE_NORM = 16.0
HGRN_HEADS = 4
HGRN_EXPAND = D_MODEL // 8
HGRN_DV = D_MODEL // 8
GLA_KW = GLA_HEADS * GLA_DK
GLA_VW = GLA_HEADS * GLA_DV
HGRN_KW = HGRN_HEADS * HGRN_EXPAND
HGRN_VW = HGRN_HEADS * HGRN_DV
IN_SIZES = (GLA_KW, GLA_KW, GLA_VW, GLA_VW, 2 * GLA_LOWRANK, HGRN_KW, 2 * HGRN_KW, HGRN_VW, HGRN_VW)
IN_WIDTH = GLA_KW * 2 + GLA_VW * 2 + 2 * GLA_LOWRANK + HGRN_KW * 3 + HGRN_VW * 2
FFN_HIDDEN = ((8 * D_MODEL // 3 + 127) // 128) * 128
CONV_W = 3
CHUNK = 16
EPS = 1e-6

kernel_name = 'hybrid_gla_hgrn2_diffusion_step'


def rmsnorm(x, g):
    xf = x.astype(jnp.float32)
    y = xf * lax.rsqrt(jnp.mean(xf * xf, axis=-1, keepdims=True) + EPS)
    return (y * g.astype(jnp.float32)).astype(x.dtype)


def chunk_gated_linear(q, k, v, log_f, s0):
    B, H, T, DK = q.shape
    DV = v.shape[-1]
    N = T // CHUNK
    f32 = jnp.float32
    q = q.astype(f32).reshape(B, H, N, CHUNK, DK)
    k = k.astype(f32).reshape(B, H, N, CHUNK, DK)
    v = v.astype(f32).reshape(B, H, N, CHUNK, DV)
    b = jnp.cumsum(log_f.astype(f32).reshape(B, H, N, CHUNK, DK), axis=3)
    b_last = b[:, :, :, -1:, :]
    lower = jnp.tril(jnp.ones((CHUNK, CHUNK), dtype=bool))[:, :, None]
    rel = b[:, :, :, :, None, :] - b[:, :, :, None, :, :]
    decay = jnp.exp(jnp.where(lower, rel, -jnp.inf))
    scores = jnp.einsum('bhnid,bhnijd,bhnjd->bhnij', q, decay, k)
    o_intra = jnp.einsum('bhnij,bhnjv->bhniv', scores, v)
    q_in = q * jnp.exp(b)
    k_out = k * jnp.exp(b_last - b)
    f_chunk = jnp.exp(b_last[:, :, :, 0, :])

    def step(S, xs):
        qc, kc, vc, fc = xs
        o = jnp.einsum('bhcd,bhdv->bhcv', qc, S)
        S = fc[..., None] * S + jnp.einsum('bhcd,bhcv->bhdv', kc, vc)
        return S, o

    xs = (jnp.moveaxis(q_in, 2, 0), jnp.moveaxis(k_out, 2, 0), jnp.moveaxis(v, 2, 0), jnp.moveaxis(f_chunk, 2, 0))
    s_final, o_inter = lax.scan(step, s0.astype(f32), xs)
    o = o_intra + jnp.moveaxis(o_inter, 0, 2)
    return o.reshape(B, H, T, DV), s_final


def bidir_scan(q, k_fwd, k_bwd, v, logf_fwd, logf_bwd, s0_fwd, s0_bwd):
    flip = lambda t: jnp.flip(t, axis=2)
    o_f, s_f = chunk_gated_linear(q, k_fwd, v, logf_fwd, s0_fwd)
    o_b, s_b = chunk_gated_linear(flip(q), flip(k_bwd), flip(v), flip(logf_bwd), s0_bwd)
    return o_f + flip(o_b), s_f, s_b


def to_heads(t, n):
    B, T, W = t.shape
    return t.reshape(B, T, n, W // n).transpose(0, 2, 1, 3)


def hybrid_mixer(h, s_gla, s_hgrn, w_in, w_gla_up, b_gla, lb, gla_norm, hgrn_norm, w_out):
    B, T, _ = h.shape
    offsets = np.cumsum(IN_SIZES)[:-1].tolist()
    q_a, k_a, v_a, g_a, lr_a, q_b, f_b, i_b, g_b = jnp.split(h @ w_in, offsets, axis=-1)
    lr = lr_a.reshape(B, T, 2, GLA_LOWRANK)
    alpha_logit = jnp.einsum('btdr,drk->dbtk', lr, w_gla_up) + b_gla[:, None, None, :]
    log_alpha = jax.nn.log_sigmoid(alpha_logit.astype(jnp.float32)) / GLA_GATE_NORM
    qa = to_heads(q_a, GLA_HEADS) * (GLA_DK ** -0.5)
    ka = to_heads(k_a, GLA_HEADS)
    va = to_heads(v_a, GLA_HEADS)
    o_a, sa_f, sa_b = bidir_scan(qa, ka, ka, va, to_heads(log_alpha[0], GLA_HEADS), to_heads(log_alpha[1], GLA_HEADS), s_gla[:, 0], s_gla[:, 1])
    o_a = rmsnorm(o_a.transpose(0, 2, 1, 3).astype(h.dtype), gla_norm) * jax.nn.silu(g_a).reshape(B, T, GLA_HEADS, GLA_DV)
    f_raw = f_b.reshape(B, T, 2, HGRN_KW).astype(jnp.float32)
    log_f = jnp.logaddexp(jnp.log(lb), jnp.log1p(-lb) + jax.nn.log_sigmoid(f_raw))
    i_gate = -jnp.expm1(log_f)
    qb = to_heads(q_b, HGRN_HEADS)
    vb = to_heads(i_b, HGRN_HEADS)
    o_h, sb_f, sb_b = bidir_scan(qb, to_heads(i_gate[:, :, 0], HGRN_HEADS), to_heads(i_gate[:, :, 1], HGRN_HEADS), vb, to_heads(log_f[:, :, 0], HGRN_HEADS), to_heads(log_f[:, :, 1], HGRN_HEADS), s_hgrn[:, 0], s_hgrn[:, 1])
    o_h = rmsnorm(o_h.transpose(0, 2, 1, 3).astype(h.dtype), hgrn_norm) * jax.nn.silu(g_b).reshape(B, T, HGRN_HEADS, HGRN_DV)
    merged = jnp.concatenate([o_a.reshape(B, T, GLA_VW), o_h.reshape(B, T, HGRN_VW)], axis=-1)
    new_gla = jnp.stack([sa_f, sa_b], axis=1).astype(s_gla.dtype)
    new_hgrn = jnp.stack([sb_f, sb_b], axis=1).astype(s_hgrn.dtype)
    return merged @ w_out, new_gla, new_hgrn


def dw_conv(u, conv_w, conv_b, grid):
    B, T, C = u.shape
    if grid:
        rows = T // GRID_W
        img = u.reshape(B, rows, GRID_W, C)
        out = lax.conv_general_dilated(img, conv_w[:, :, None, :], (1, 1), 'SAME', dimension_numbers=('NHWC', 'HWIO', 'NHWC'), feature_group_count=C)
        out = out.reshape(B, T, C)
    else:
        out = lax.conv_general_dilated(u, conv_w[CONV_W // 2][:, None, :], (1,), 'SAME', dimension_numbers=('NWC', 'WIO', 'NWC'), feature_group_count=C)
    return out + conv_b


def conv_ffn(h, w_up, conv_w, conv_b, w_down, grid):
    u = dw_conv(h @ w_up, conv_w, conv_b, grid)
    gate, up = jnp.split(u, 2, axis=-1)
    return (jax.nn.silu(gate) * up) @ w_down


def trunk_layer(x, mod, s_gla, s_hgrn, norm1, norm2, w_in, w_gla_up, b_gla, lb, gla_norm, hgrn_norm, w_out, w_ffn_up, ffn_conv, b_ffn_conv, w_ffn_down, grid):
    shift1, scale1, gate1, shift2, scale2, gate2 = jnp.split(mod, 6, axis=-1)
    h = rmsnorm(x, norm1) * (1 + scale1) + shift1
    mix, new_gla, new_hgrn = hybrid_mixer(h, s_gla, s_hgrn, w_in, w_gla_up, b_gla, lb, gla_norm, hgrn_norm, w_out)
    x = x + gate1 * mix
    h = rmsnorm(x, norm2) * (1 + scale2) + shift2
    x = x + gate2 * conv_ffn(h, w_ffn_up, ffn_conv, b_ffn_conv, w_ffn_down, grid)
    return x, new_gla, new_hgrn


def setup_inputs(seed: int = 0) -> dict:
    key = jax.random.key(seed)
    ks = jax.random.split(key, 24)
    n = lambda i, shape: jax.random.normal(ks[i], shape, jnp.float32)
    return {
        'x_prompt': n(0, (BATCH, SEQ, D_MODEL)),
        'x_sample': n(1, (DEC_BATCH, DEC_SEQ, D_MODEL)),
        'state_gla': 0.5 * n(2, (DEC_BATCH, DEPTH, 2, GLA_HEADS, GLA_DK, GLA_DV)),
        'state_hgrn': 0.5 * n(3, (DEC_BATCH, DEPTH, 2, HGRN_HEADS, HGRN_EXPAND, HGRN_DV)),
        'c': n(4, (DEC_BATCH, D_MODEL)),
        'c_ctx': n(5, (D_MODEL,)),
        'w_ada': 0.5 * D_MODEL ** -0.5 * n(6, (DEPTH, D_MODEL, 6 * D_MODEL)),
        'b_ada': 0.02 * n(7, (DEPTH, 6 * D_MODEL)),
        'norm1': 1.0 + 0.02 * n(8, (DEPTH, D_MODEL)),
        'norm2': 1.0 + 0.02 * n(9, (DEPTH, D_MODEL)),
        'w_in': D_MODEL ** -0.5 * n(10, (DEPTH, D_MODEL, IN_WIDTH)),
        'w_gla_up': GLA_LOWRANK ** -0.5 * n(11, (DEPTH, 2, GLA_LOWRANK, GLA_KW)),
        'b_gla': 0.1 * n(12, (DEPTH, 2, GLA_KW)),
        'hgrn_lb': 0.5 * n(13, (DEPTH + 1, 2, HGRN_KW)),
        'gla_norm': 1.0 + 0.02 * n(14, (DEPTH, GLA_DV)),
        'hgrn_norm': 1.0 + 0.02 * n(15, (DEPTH, HGRN_DV)),
        'w_out': D_MODEL ** -0.5 * n(16, (DEPTH, D_MODEL, D_MODEL)),
        'w_ffn_up': D_MODEL ** -0.5 * n(17, (DEPTH, D_MODEL, 2 * FFN_HIDDEN)),
        'ffn_conv': (CONV_W * CONV_W) ** -0.5 * n(18, (DEPTH, CONV_W, CONV_W, 2 * FFN_HIDDEN)),
        'b_ffn_conv': 0.02 * n(19, (DEPTH, 2 * FFN_HIDDEN)),
        'w_ffn_down': FFN_HIDDEN ** -0.5 * n(20, (DEPTH, FFN_HIDDEN, D_MODEL)),
        'final_norm': 1.0 + 0.02 * n(21, (D_MODEL,)),
    }


def reference(x_prompt, x_sample, state_gla, state_hgrn, c, c_ctx, w_ada, b_ada, norm1, norm2, w_in, w_gla_up, b_gla, hgrn_lb, gla_norm, hgrn_norm, w_out, w_ffn_up, ffn_conv, b_ffn_conv, w_ffn_down, final_norm):
    lb_all = jnp.cumsum(jax.nn.softmax(hgrn_lb.astype(jnp.float32), axis=0), axis=0)
    bp = x_prompt.shape[0]
    zero_gla = jnp.zeros((bp, 2, GLA_HEADS, GLA_DK, GLA_DV), state_gla.dtype)
    zero_hgrn = jnp.zeros((bp, 2, HGRN_HEADS, HGRN_EXPAND, HGRN_DV), state_hgrn.dtype)
    xp, xs = x_prompt, x_sample
    gla_states, hgrn_states = [], []
    for l in range(DEPTH):
        mod_ctx = (jax.nn.silu(c_ctx) @ w_ada[l] + b_ada[l])[None, None, :]
        mod_lat = (jax.nn.silu(c) @ w_ada[l] + b_ada[l])[:, None, :]
        layer_w = (norm1[l], norm2[l], w_in[l], w_gla_up[l], b_gla[l], lb_all[l], gla_norm[l], hgrn_norm[l], w_out[l], w_ffn_up[l], ffn_conv[l], b_ffn_conv[l], w_ffn_down[l])
        xp, sg, sh = trunk_layer(xp, mod_ctx, zero_gla, zero_hgrn, *layer_w, grid=False)
        gla_states.append(sg)
        hgrn_states.append(sh)
        xs, _, _ = trunk_layer(xs, mod_lat, state_gla[:, l], state_hgrn[:, l], *layer_w, grid=True)
    y_prompt = rmsnorm(xp, final_norm)
    y_sample = rmsnorm(xs, final_norm)
    new_state_gla = jnp.stack(gla_states, axis=1)
    new_state_hgrn = jnp.stack(hgrn_states, axis=1)
    return (y_prompt, y_sample, new_state_gla, new_state_hgrn)
```

```python
import functools

import jax
import jax.numpy as jnp
from jax import lax
from jax.experimental import pallas as pl
from jax.experimental.pallas import tpu as pltpu

F32 = jnp.float32
BF16 = jnp.bfloat16

D_MODEL = 1024
N_HEADS = 8
GLA_HEADS = 4
GLA_DK = 64
HEAD_W = 128
GLA_LOWRANK = 16
GLA_GATE_NORM = 16.0
FFN_HIDDEN = 2816
GRID_W = 64
EPS = 1e-6

GROUP_ROWS = 2048
TILE = 128
CHUNK = 16
CHUNKS_PER_TILE = TILE // CHUNK
TILES = GROUP_ROWS // TILE
ROW_BLOCK = 256
PROJ_W = 5 * HEAD_W
FFN_TILE = 256
FFN_STEPS = FFN_HIDDEN // FFN_TILE
CONV_PAD = 72
VMEM_LIMIT = 56 * 1024 * 1024

NT_DIMS = (((1,), (1,)), ((), ()))
TN_DIMS = (((0,), (0,)), ((), ()))


def _row_loop(n_rows, block, body):
    def step(i, carry):
        body(pl.ds(pl.multiple_of(i * block, block), block))
        return carry
    lax.fori_loop(0, n_rows // block, step, 0)


def _rms(x):
    return x * lax.rsqrt(jnp.mean(x * x, axis=-1, keepdims=True) + EPS)


def _silu(x):
    return x / (1.0 + jnp.exp(-x))


def _mod_kernel(c_ref, w_ref, b_ref, o_ref):
    a = _silu(c_ref[...]).astype(BF16)
    o_ref[...] = jnp.dot(a, w_ref[...].astype(BF16), preferred_element_type=F32) + b_ref[...]


def _modulation(cvecs, w_ada, b_ada):
    n = w_ada.shape[1]
    tn = 512
    return pl.pallas_call(
        _mod_kernel,
        grid=(n // tn,),
        in_specs=[pl.BlockSpec((8, D_MODEL), lambda j: (0, 0)),
                  pl.BlockSpec((D_MODEL, tn), lambda j: (0, j)),
                  pl.BlockSpec((1, tn), lambda j: (0, j))],
        out_specs=pl.BlockSpec((8, tn), lambda j: (0, j)),
        out_shape=jax.ShapeDtypeStruct((8, n), F32),
        name="modulation",
    )(cvecs, w_ada, b_ada)


def _chunk_phase(q, k, v_bf, g, rev):
    row = lax.broadcasted_iota(jnp.int32, (TILE, HEAD_W), 0)
    col = lax.broadcasted_iota(jnp.int32, (TILE, HEAD_W), 1)
    cpos = row & (CHUNK - 1)
    f = jnp.exp(g)
    if rev:
        fz = jnp.where(cpos == CHUNK - 1, 0.0, f)
        diff = row - col
        edge_row = 0
    else:
        fz = jnp.where(cpos == 0, 0.0, f)
        diff = col - row
        edge_row = CHUNK - 1
    b = g
    for s in (1, 2, 4, 8):
        if rev:
            b = b + jnp.where(cpos <= CHUNK - 1 - s, pltpu.roll(b, TILE - s, 0), 0.0)
        else:
            b = b + jnp.where(cpos >= s, pltpu.roll(b, s, 0), 0.0)
    tot_rows = [b[j * CHUNK + edge_row:j * CHUNK + edge_row + 1, :] for j in range(CHUNKS_PER_TILE)]
    b_tot = jnp.concatenate([jnp.broadcast_to(r, (CHUNK, HEAD_W)) for r in tot_rows], axis=0)
    fc = jnp.exp(jnp.concatenate(tot_rows, axis=0))
    q_in = q * jnp.exp(b)
    k_out = (k * jnp.exp(b_tot - b)).astype(BF16)
    w = k
    a = jnp.zeros((TILE, TILE), F32)
    shift = TILE - 1 if rev else 1
    for d in range(CHUNK):
        if d:
            w = pltpu.roll(w, shift, 0) * fz
        s = jnp.sum(q * w, axis=1, keepdims=True)
        a = jnp.where(diff == -d, s, a)
    o_intra = jnp.dot(a.astype(BF16), v_bf, preferred_element_type=F32)
    return o_intra, q_in, k_out, fc


def _make_mixer_kernel(seqs_per_group, has_state_in, has_state_out):
    tiles_per_seq = TILES // seqs_per_group

    def kernel(*refs):
        it = iter(refs)
        x_ref = next(it); mod_ref = next(it); n1_ref = next(it); w_ref = next(it)
        wup_ref = next(it); bg_ref = next(it); lb_ref = next(it); hn_ref = next(it); wout_ref = next(it)
        s0_ref = next(it) if has_state_in else None
        x1_ref = next(it)
        st_ref = next(it) if has_state_out else None
        (h_scr, proj_scr, g_scr, k_scr, qin_scr, kout_scr, qt_scr, vb_scr, fc_scr, o_scr,
         sloc_scr, gt_scr) = it

        hd = pl.program_id(1)

        @pl.when(hd == 0)
        def _():
            shift1 = mod_ref[0, :, 0:D_MODEL]
            scale1 = mod_ref[0, :, D_MODEL:2 * D_MODEL]
            n1 = n1_ref[...]

            def norm_rows(rows):
                h = _rms(x_ref[rows, :]) * n1
                h_scr[rows, :] = (h * (1.0 + scale1) + shift1).astype(BF16)
                x1_ref[rows, :] = jnp.zeros((ROW_BLOCK, D_MODEL), F32)
            _row_loop(GROUP_ROWS, ROW_BLOCK, norm_rows)

        def proj_rows(rows):
            proj_scr[rows, :] = jnp.dot(h_scr[rows, :], w_ref[0], preferred_element_type=F32)
        _row_loop(GROUP_ROWS, ROW_BLOCK, proj_rows)

        @pl.when(hd < GLA_HEADS)
        def _():
            def gla_rows(rows):
                proj_scr[rows, 0:HEAD_W] = proj_scr[rows, 0:HEAD_W] * (GLA_DK ** -0.5)
                k = proj_scr[rows, HEAD_W:2 * HEAD_W]
                lr = proj_scr[rows, 2 * HEAD_W:3 * HEAD_W].astype(BF16)
                for d in range(2):
                    z = jnp.dot(lr, wup_ref[0, d], preferred_element_type=F32) + bg_ref[0, d:d + 1, :]
                    log_sig = jnp.minimum(z, 0.0) - jnp.log1p(jnp.exp(-jnp.abs(z)))
                    g_scr[d, rows, :] = log_sig * (1.0 / GLA_GATE_NORM)
                    k_scr[d, rows, :] = k
            _row_loop(GROUP_ROWS, ROW_BLOCK, gla_rows)

        @pl.when(hd >= GLA_HEADS)
        def _():
            def hgrn_rows(rows):
                for d in range(2):
                    a0 = lb_ref[0, d:d + 1, :]
                    a1 = lb_ref[0, 2 + d:3 + d, :]
                    m = jnp.maximum(a0, a1)
                    e0 = jnp.exp(a0 - m)
                    e1 = jnp.exp(a1 - m)
                    lb = e0 / (e0 + e1)
                    xr = proj_scr[rows, (1 + d) * HEAD_W:(2 + d) * HEAD_W]
                    f = lb + (1.0 - lb) / (1.0 + jnp.exp(-xr))
                    g_scr[d, rows, :] = jnp.log(f)
                    k_scr[d, rows, :] = (1.0 - lb) / (1.0 + jnp.exp(xr))
            _row_loop(GROUP_ROWS, ROW_BLOCK, hgrn_rows)

        def tile_body(t, carry):
            rows = pl.ds(pl.multiple_of(t * TILE, TILE), TILE)
            q = proj_scr[rows, 0:HEAD_W]
            v_bf = proj_scr[rows, 3 * HEAD_W:4 * HEAD_W].astype(BF16)
            vb_scr[rows, :] = v_bf
            o_sum = None
            for d in range(2):
                o_intra, q_in, k_out, fc = _chunk_phase(q, k_scr[d, rows, :], v_bf, g_scr[d, rows, :], rev=(d == 1))
                qin_scr[d, rows, :] = q_in
                kout_scr[d, rows, :] = k_out
                fc_scr[d, pl.ds(pl.multiple_of(t * CHUNKS_PER_TILE, CHUNKS_PER_TILE), CHUNKS_PER_TILE), :] = fc
                o_sum = o_intra if o_sum is None else o_sum + o_intra
            o_scr[rows, :] = o_sum
            return carry
        lax.fori_loop(0, TILES, tile_body, 0)

        sloc_scr[...] = jnp.zeros_like(sloc_scr)
        gt_scr[...] = jnp.ones_like(gt_scr)

        def chunk_body(c, carry):
            for d in range(2):
                cc = c if d == 0 else CHUNKS_PER_TILE - 1 - c
                for t in range(TILES):
                    rows = pl.ds(pl.multiple_of(t * TILE + cc * CHUNK, CHUNK), CHUNK)
                    q_in = qin_scr[d, rows, :]
                    s_loc = sloc_scr[d, t]
                    o_loc = lax.dot_general(q_in.astype(BF16), s_loc.astype(BF16), NT_DIMS, preferred_element_type=F32)
                    o_scr[rows, :] += o_loc
                    u_t = lax.dot_general(vb_scr[rows, :], kout_scr[d, rows, :], TN_DIMS, preferred_element_type=F32)
                    fc = fc_scr[d, pl.ds(t * CHUNKS_PER_TILE + cc, 1), :]
                    sloc_scr[d, t] = s_loc * fc + u_t
                    gt = gt_scr[d * TILES + t:d * TILES + t + 1, :]
                    qt_scr[d, rows, :] = (q_in * gt).astype(BF16)
                    gt_scr[d * TILES + t:d * TILES + t + 1, :] = gt * fc
            return carry
        lax.fori_loop(0, CHUNKS_PER_TILE, chunk_body, 0)

        for s in range(seqs_per_group):
            for d in range(2):
                if has_state_in:
                    s_run = s0_ref[s, d, 0].T
                else:
                    s_run = jnp.zeros((HEAD_W, HEAD_W), F32)
                order = range(tiles_per_seq) if d == 0 else range(tiles_per_seq - 1, -1, -1)
                for tt in order:
                    t = s * tiles_per_seq + tt
                    rows = pl.ds(t * TILE, TILE)
                    o_scr[rows, :] += lax.dot_general(qt_scr[d, rows, :], s_run.astype(BF16), NT_DIMS,
                                                      preferred_element_type=F32)
                    s_run = s_run * gt_scr[d * TILES + t:d * TILES + t + 1, :] + sloc_scr[d, t]
                if has_state_out:
                    st_ref[s, d, 0] = s_run.T

        def out_rows(rows):
            o = _rms(o_scr[rows, :]) * hn_ref[0]
            merged = (o * _silu(proj_scr[rows, 4 * HEAD_W:5 * HEAD_W])).astype(BF16)
            x1_ref[rows, :] += jnp.dot(merged, wout_ref[0], preferred_element_type=F32)
        _row_loop(GROUP_ROWS, ROW_BLOCK, out_rows)

        @pl.when(hd == N_HEADS - 1)
        def _():
            gate1 = mod_ref[0, :, 2 * D_MODEL:3 * D_MODEL]

            def res_rows(rows):
                x1_ref[rows, :] = x_ref[rows, :] + gate1 * x1_ref[rows, :]
            _row_loop(GROUP_ROWS, ROW_BLOCK, res_rows)

    return kernel


def _mixer(x, mod3, mod_row, norm1, w_heads, wup, bgl, lbl, hnorm, wout_h, s0, seqs_per_group, want_states):
    groups = x.shape[0] // GROUP_ROWS
    n_seq = groups * seqs_per_group
    has_state_in = s0 is not None
    once = pl.Buffered(1)
    in_specs = [
        pl.BlockSpec((GROUP_ROWS, D_MODEL), lambda g, h: (g, 0), pipeline_mode=once),
        pl.BlockSpec((1, 1, 6 * D_MODEL), lambda g, h: (mod_row(g), 0, 0)),
        pl.BlockSpec((1, D_MODEL), lambda g, h: (0, 0)),
        pl.BlockSpec((1, D_MODEL, PROJ_W), lambda g, h: (h, 0, 0)),
        pl.BlockSpec((1, 2, HEAD_W, HEAD_W), lambda g, h: (h, 0, 0, 0)),
        pl.BlockSpec((1, 2, HEAD_W), lambda g, h: (h, 0, 0)),
        pl.BlockSpec((1, 4, HEAD_W), lambda g, h: (h, 0, 0)),
        pl.BlockSpec((1, 1, HEAD_W), lambda g, h: (h, 0, 0)),
        pl.BlockSpec((1, HEAD_W, D_MODEL), lambda g, h: (h, 0, 0)),
    ]
    args = [x, mod3, norm1, w_heads, wup, bgl, lbl, hnorm, wout_h]
    if has_state_in:
        in_specs.append(pl.BlockSpec((seqs_per_group, 2, 1, HEAD_W, HEAD_W), lambda g, h: (g, 0, h, 0, 0)))
        args.append(s0)
    out_specs = [pl.BlockSpec((GROUP_ROWS, D_MODEL), lambda g, h: (g, 0), pipeline_mode=once)]
    out_shape = [jax.ShapeDtypeStruct(x.shape, F32)]
    if want_states:
        out_specs.append(pl.BlockSpec((seqs_per_group, 2, 1, HEAD_W, HEAD_W), lambda g, h: (g, 0, h, 0, 0)))
        out_shape.append(jax.ShapeDtypeStruct((n_seq, 2, N_HEADS, HEAD_W, HEAD_W), F32))
    scratch = [
        pltpu.VMEM((GROUP_ROWS, D_MODEL), BF16),
        pltpu.VMEM((GROUP_ROWS, PROJ_W), F32),
        pltpu.VMEM((2, GROUP_ROWS, HEAD_W), F32),
        pltpu.VMEM((2, GROUP_ROWS, HEAD_W), F32),
        pltpu.VMEM((2, GROUP_ROWS, HEAD_W), F32),
        pltpu.VMEM((2, GROUP_ROWS, HEAD_W), BF16),
        pltpu.VMEM((2, GROUP_ROWS, HEAD_W), BF16),
        pltpu.VMEM((GROUP_ROWS, HEAD_W), BF16),
        pltpu.VMEM((2, GROUP_ROWS // CHUNK, HEAD_W), F32),
        pltpu.VMEM((GROUP_ROWS, HEAD_W), F32),
        pltpu.VMEM((2, TILES, HEAD_W, HEAD_W), F32),
        pltpu.VMEM((2 * TILES, HEAD_W), F32),
    ]
    outs = pl.pallas_call(
        _make_mixer_kernel(seqs_per_group, has_state_in, want_states),
        grid=(groups, N_HEADS),
        in_specs=in_specs,
        out_specs=out_specs,
        out_shape=out_shape,
        scratch_shapes=scratch,
        compiler_params=pltpu.CompilerParams(dimension_semantics=("arbitrary", "arbitrary"),
                                             vmem_limit_bytes=VMEM_LIMIT),
        name="mixer_ctx" if want_states else "mixer_lat",
    )(*args)
    return outs


def _make_ffn_kernel(grid_conv, seq_len):
    def kernel(x1_ref, mod_ref, n2_ref, wg_ref, wu_ref, cwg_ref, cwu_ref, cbg_ref, cbu_ref, wd_ref, fn_ref,
               y_ref, h_scr, pg_scr, pu_scr):
        j = pl.program_id(1)

        @pl.when(j == 0)
        def _():
            shift2 = mod_ref[0, :, 3 * D_MODEL:4 * D_MODEL]
            scale2 = mod_ref[0, :, 4 * D_MODEL:5 * D_MODEL]
            n2 = n2_ref[...]

            def norm_rows(rows):
                h = _rms(x1_ref[rows, :]) * n2
                h_scr[rows, :] = (h * (1.0 + scale2) + shift2).astype(BF16)
                y_ref[rows, :] = jnp.zeros((ROW_BLOCK, D_MODEL), F32)
            _row_loop(GROUP_ROWS, ROW_BLOCK, norm_rows)
            zeros = jnp.zeros((CONV_PAD, FFN_TILE), F32)
            for scr in (pg_scr, pu_scr):
                scr[0:CONV_PAD, :] = zeros
                scr[CONV_PAD + GROUP_ROWS:2 * CONV_PAD + GROUP_ROWS, :] = zeros

        def up_rows(rows):
            h = h_scr[rows, :]
            dst = pl.ds(pl.multiple_of(rows.start + CONV_PAD, 8), ROW_BLOCK)
            pg_scr[dst, :] = jnp.dot(h, wg_ref[...], preferred_element_type=F32)
            pu_scr[dst, :] = jnp.dot(h, wu_ref[...], preferred_element_type=F32)
        _row_loop(GROUP_ROWS, ROW_BLOCK, up_rows)

        def conv(scr, cw_ref, cb_ref, start, pos):
            dys = (-1, 0, 1) if grid_conv else (0,)
            width = GRID_W if grid_conv else seq_len
            halo = 8
            win = ROW_BLOCK + 2 * halo
            wins = [scr[pl.ds(pl.multiple_of(start + CONV_PAD + dy * GRID_W - halo, 8), win), :] for dy in dys]
            acc = None
            for dx in (-1, 0, 1):
                part = None
                for dy, w in zip(dys, wins):
                    tap = cw_ref[(dy + 1) * 3 + (dx + 1):(dy + 1) * 3 + (dx + 2), :]
                    part = w * tap if part is None else part + w * tap
                if dx:
                    part = pltpu.roll(part, 1 if dx == -1 else win - 1, 0)
                part = part[halo:halo + ROW_BLOCK]
                if dx == -1:
                    part = jnp.where(pos != 0, part, 0.0)
                elif dx == 1:
                    part = jnp.where(pos != width - 1, part, 0.0)
                acc = part if acc is None else acc + part
            return acc + cb_ref[...]

        def down_rows(rows):
            ridx = rows.start + lax.broadcasted_iota(jnp.int32, (ROW_BLOCK, FFN_TILE), 0)
            pos = ridx & ((GRID_W if grid_conv else seq_len) - 1)
            cg = conv(pg_scr, cwg_ref, cbg_ref, rows.start, pos)
            cu = conv(pu_scr, cwu_ref, cbu_ref, rows.start, pos)
            act = (_silu(cg) * cu).astype(BF16)
            y_ref[rows, :] += jnp.dot(act, wd_ref[...], preferred_element_type=F32)
        _row_loop(GROUP_ROWS, ROW_BLOCK, down_rows)

        @pl.when(j == FFN_STEPS - 1)
        def _():
            gate2 = mod_ref[0, :, 5 * D_MODEL:6 * D_MODEL]
            fn = fn_ref[...]

            def res_rows(rows):
                y_ref[rows, :] = _rms(x1_ref[rows, :] + gate2 * y_ref[rows, :]) * fn
            _row_loop(GROUP_ROWS, ROW_BLOCK, res_rows)

    return kernel


def _conv_ffn(x1, mod3, mod_row, norm2, w_up, conv_w, conv_b, w_down, final_norm, grid_conv, seq_len):
    groups = x1.shape[0] // GROUP_ROWS
    once = pl.Buffered(1)
    in_specs = [
        pl.BlockSpec((GROUP_ROWS, D_MODEL), lambda g, j: (g, 0), pipeline_mode=once),
        pl.BlockSpec((1, 1, 6 * D_MODEL), lambda g, j: (mod_row(g), 0, 0)),
        pl.BlockSpec((1, D_MODEL), lambda g, j: (0, 0)),
        pl.BlockSpec((D_MODEL, FFN_TILE), lambda g, j: (0, j)),
        pl.BlockSpec((D_MODEL, FFN_TILE), lambda g, j: (0, FFN_STEPS + j)),
        pl.BlockSpec((9, FFN_TILE), lambda g, j: (0, j)),
        pl.BlockSpec((9, FFN_TILE), lambda g, j: (0, FFN_STEPS + j)),
        pl.BlockSpec((1, FFN_TILE), lambda g, j: (0, j)),
        pl.BlockSpec((1, FFN_TILE), lambda g, j: (0, FFN_STEPS + j)),
        pl.BlockSpec((FFN_TILE, D_MODEL), lambda g, j: (j, 0)),
        pl.BlockSpec((1, D_MODEL), lambda g, j: (0, 0)),
    ]
    scratch = [
        pltpu.VMEM((GROUP_ROWS, D_MODEL), BF16),
        pltpu.VMEM((GROUP_ROWS + 2 * CONV_PAD, FFN_TILE), F32),
        pltpu.VMEM((GROUP_ROWS + 2 * CONV_PAD, FFN_TILE), F32),
    ]
    return pl.pallas_call(
        _make_ffn_kernel(grid_conv, seq_len),
        grid=(groups, FFN_STEPS),
        in_specs=in_specs,
        out_specs=pl.BlockSpec((GROUP_ROWS, D_MODEL), lambda g, j: (g, 0), pipeline_mode=once),
        out_shape=jax.ShapeDtypeStruct(x1.shape, F32),
        scratch_shapes=scratch,
        compiler_params=pltpu.CompilerParams(dimension_semantics=("arbitrary", "arbitrary"),
                                             vmem_limit_bytes=VMEM_LIMIT),
        name="conv_ffn_lat" if grid_conv else "conv_ffn_ctx",
    )(x1, mod3, norm2, w_up, w_up, conv_w, conv_w, conv_b, conv_b, w_down, final_norm)


def _head_layout(w_in, w_gla_up, b_gla, hgrn_lb, gla_norm, hgrn_norm, w_out):
    gla_kw = GLA_HEADS * GLA_DK
    vw = GLA_HEADS * HEAD_W
    o_qa, o_ka, o_va, o_ga = 0, gla_kw, 2 * gla_kw, 2 * gla_kw + vw
    o_lr = o_ga + vw
    o_qb = o_lr + 2 * GLA_LOWRANK
    o_fb = o_qb + vw
    o_ib = o_fb + 2 * vw
    o_gb = o_ib + vw
    wb = w_in.astype(BF16)
    zeros = lambda n: jnp.zeros((D_MODEL, n), BF16)
    heads = []
    for h in range(GLA_HEADS):
        heads.append(jnp.concatenate([
            wb[:, o_qa + h * GLA_DK:o_qa + (h + 1) * GLA_DK], zeros(HEAD_W - GLA_DK),
            wb[:, o_ka + h * GLA_DK:o_ka + (h + 1) * GLA_DK], zeros(HEAD_W - GLA_DK),
            wb[:, o_lr:o_lr + 2 * GLA_LOWRANK], zeros(HEAD_W - 2 * GLA_LOWRANK),
            wb[:, o_va + h * HEAD_W:o_va + (h + 1) * HEAD_W],
            wb[:, o_ga + h * HEAD_W:o_ga + (h + 1) * HEAD_W]], axis=1))
    for h in range(N_HEADS - GLA_HEADS):
        sl = slice(h * HEAD_W, (h + 1) * HEAD_W)
        heads.append(jnp.concatenate([
            wb[:, o_qb:o_qb + vw][:, sl], wb[:, o_fb:o_fb + vw][:, sl], wb[:, o_fb + vw:o_fb + 2 * vw][:, sl],
            wb[:, o_ib:o_ib + vw][:, sl], wb[:, o_gb:o_gb + vw][:, sl]], axis=1))
    w_heads = jnp.stack(heads, axis=0)

    wup = jnp.zeros((N_HEADS, 2, HEAD_W, HEAD_W), BF16)
    bgl = jnp.zeros((N_HEADS, 2, HEAD_W), F32)
    for h in range(GLA_HEADS):
        for d in range(2):
            wup = wup.at[h, d, d * GLA_LOWRANK:(d + 1) * GLA_LOWRANK, 0:GLA_DK].set(
                w_gla_up[d, :, h * GLA_DK:(h + 1) * GLA_DK].astype(BF16))
            bgl = bgl.at[h, d, 0:GLA_DK].set(b_gla[d, h * GLA_DK:(h + 1) * GLA_DK])
    lb = hgrn_lb.astype(F32).reshape(2, 2, N_HEADS - GLA_HEADS, HEAD_W)
    lbl = jnp.concatenate([jnp.zeros((GLA_HEADS, 4, HEAD_W), F32),
                           lb.transpose(2, 0, 1, 3).reshape(N_HEADS - GLA_HEADS, 4, HEAD_W)], axis=0)
    hnorm = jnp.concatenate([jnp.broadcast_to(gla_norm[None, None, :], (GLA_HEADS, 1, HEAD_W)),
                             jnp.broadcast_to(hgrn_norm[None, None, :], (N_HEADS - GLA_HEADS, 1, HEAD_W))], axis=0)
    wout_h = w_out.astype(BF16).reshape(N_HEADS, HEAD_W, D_MODEL)
    return w_heads, wup, bgl, lbl, hnorm, wout_h


def kernel(x_prompt, x_sample, state_gla, state_hgrn, c, c_ctx, w_ada, b_ada, norm1, norm2, w_in, w_gla_up, b_gla, hgrn_lb, gla_norm, hgrn_norm, w_out, w_ffn_up, ffn_conv, b_ffn_conv, w_ffn_down, final_norm):
    assert w_ada.shape[0] == 1 and hgrn_lb.shape[0] == 2, "single layer only"
    n_ctx, ctx_len, _ = x_prompt.shape
    n_lat, lat_len, _ = x_sample.shape
    assert lat_len == GROUP_ROWS and GROUP_ROWS % ctx_len == 0 and n_ctx % (GROUP_ROWS // ctx_len) == 0
    assert n_lat + 1 <= 8
    ctx_per_group = GROUP_ROWS // ctx_len

    cvecs = jnp.concatenate([c_ctx[None, :], c, jnp.zeros((8 - 1 - n_lat, D_MODEL), F32)], axis=0)
    mod3 = _modulation(cvecs, w_ada[0], b_ada).reshape(8, 1, 6 * D_MODEL)

    w_heads, wup, bgl, lbl, hnorm, wout_h = _head_layout(
        w_in[0], w_gla_up[0], b_gla[0], hgrn_lb, gla_norm[0], hgrn_norm[0], w_out[0])
    s0 = jnp.concatenate([jnp.pad(state_gla[:, 0], ((0, 0), (0, 0), (0, 0), (0, HEAD_W - GLA_DK), (0, 0))),
                          state_hgrn[:, 0]], axis=2)
    w_up = w_ffn_up[0].astype(BF16)
    w_down = w_ffn_down[0].astype(BF16)
    conv_w = ffn_conv[0].reshape(9, 2 * FFN_HIDDEN)
    conv_b = b_ffn_conv
    ctx_row = lambda g: 0
    lat_row = lambda g: g + 1

    xp = x_prompt.reshape(n_ctx * ctx_len, D_MODEL)
    xs = x_sample.reshape(n_lat * lat_len, D_MODEL)
    xp1, states = _mixer(xp, mod3, ctx_row, norm1, w_heads, wup, bgl, lbl, hnorm, wout_h, None,
                         ctx_per_group, True)
    (xs1,) = _mixer(xs, mod3, lat_row, norm1, w_heads, wup, bgl, lbl, hnorm, wout_h, s0, 1, False)
    yp = _conv_ffn(xp1, mod3, ctx_row, norm2, w_up, conv_w, conv_b, w_down, final_norm[None, :], False, ctx_len)
    ys = _conv_ffn(xs1, mod3, lat_row, norm2, w_up, conv_w, conv_b, w_down, final_norm[None, :], True, lat_len)

    new_gla = states[:, None, :, :GLA_HEADS, :GLA_DK, :]
    new_hgrn = states[:, None, :, GLA_HEADS:, :, :]
    return (yp.reshape(x_prompt.shape), ys.reshape(x_sample.shape), new_gla, new_hgrn)
```

```python
import functools

import jax
import jax.numpy as jnp
from jax import lax
from jax.experimental import pallas as pl
from jax.experimental.pallas import tpu as pltpu

F32 = jnp.float32
BF16 = jnp.bfloat16

D_MODEL = 1024
N_HEADS = 8
GLA_HEADS = 4
GLA_DK = 64
HEAD_W = 128
GLA_LOWRANK = 16
GLA_GATE_NORM = 16.0
FFN_HIDDEN = 2816
GRID_W = 64
EPS = 1e-6

GROUP_ROWS = 2048
TILE = 128
CHUNK = 16
CHUNKS_PER_TILE = TILE // CHUNK
TILE_DECAY_LIMIT = 150.0
TILES = GROUP_ROWS // TILE
ROW_BLOCK = 256
MATMUL_ROWS = 1024
PROJ_W = 5 * HEAD_W
FFN_TILE = 256
FFN_STEPS = FFN_HIDDEN // FFN_TILE
FFN_ROWS = 512
CONV_PAD = 72
VMEM_LIMIT = 56 * 1024 * 1024

NT_DIMS = (((1,), (1,)), ((), ()))
TN_DIMS = (((0,), (0,)), ((), ()))


def _aligned(x, m):
    return x if isinstance(x, int) else pl.multiple_of(x, m)


def _block_start(i, block):
    return _aligned(i * block, block)


def _row_loop(n_rows, block, body):
    def step(i, carry):
        body(pl.ds(pl.multiple_of(i * block, block), block))
        return carry
    lax.fori_loop(0, n_rows // block, step, 0)


def _rms(x):
    return x * lax.rsqrt(jnp.mean(x * x, axis=-1, keepdims=True) + EPS)


def _silu(x):
    return x / (1.0 + jnp.exp(-x))


def _mod_kernel(c_ref, w_ref, b_ref, o_ref):
    a = _silu(c_ref[...]).astype(BF16)
    o_ref[...] = jnp.dot(a, w_ref[...].astype(BF16), preferred_element_type=F32) + b_ref[...]


def _modulation(cvecs, w_ada, b_ada):
    n = w_ada.shape[1]
    tn = 512
    return pl.pallas_call(
        _mod_kernel,
        grid=(n // tn,),
        in_specs=[pl.BlockSpec((8, D_MODEL), lambda j: (0, 0)),
                  pl.BlockSpec((D_MODEL, tn), lambda j: (0, j)),
                  pl.BlockSpec((1, tn), lambda j: (0, j))],
        out_specs=pl.BlockSpec((8, tn), lambda j: (0, j)),
        out_shape=jax.ShapeDtypeStruct((8, n), F32),
        name="modulation",
    )(cvecs, w_ada, b_ada)


def _chunk_phase(q, k, v_bf, g, rev):
    row = lax.broadcasted_iota(jnp.int32, (TILE, HEAD_W), 0)
    col = lax.broadcasted_iota(jnp.int32, (TILE, HEAD_W), 1)
    cpos = row & (CHUNK - 1)
    f = jnp.exp(g)
    if rev:
        fz = jnp.where(cpos == CHUNK - 1, 0.0, f)
        diff = row - col
        edge_row = 0
    else:
        fz = jnp.where(cpos == 0, 0.0, f)
        diff = col - row
        edge_row = CHUNK - 1
    b = g
    for s in (1, 2, 4, 8):
        if rev:
            b = b + jnp.where(cpos <= CHUNK - 1 - s, pltpu.roll(b, TILE - s, 0), 0.0)
        else:
            b = b + jnp.where(cpos >= s, pltpu.roll(b, s, 0), 0.0)
    tot_rows = [b[j * CHUNK + edge_row:j * CHUNK + edge_row + 1, :] for j in range(CHUNKS_PER_TILE)]
    b_tot = jnp.concatenate([jnp.broadcast_to(r, (CHUNK, HEAD_W)) for r in tot_rows], axis=0)
    fc = jnp.exp(jnp.concatenate(tot_rows, axis=0))
    q_in = q * jnp.exp(b)
    k_out = (k * jnp.exp(b_tot - b)).astype(BF16)
    w = k
    a = jnp.zeros((TILE, TILE), F32)
    shift = TILE - 1 if rev else 1
    for d in range(CHUNK):
        if d:
            w = pltpu.roll(w, shift, 0) * fz
        s = jnp.sum(q * w, axis=1, keepdims=True)
        a = jnp.where(diff == -d, s, a)
    o_intra = jnp.dot(a.astype(BF16), v_bf, preferred_element_type=F32)
    return o_intra, q_in, k_out, fc


def _tile_cumsum(g, rev):
    row = lax.broadcasted_iota(jnp.int32, (TILE, TILE), 0)
    col = lax.broadcasted_iota(jnp.int32, (TILE, TILE), 1)
    tri = jnp.where((col >= row) if rev else (col <= row), 1.0, 0.0).astype(BF16)
    hi = g.astype(BF16)
    rest = g - hi.astype(F32)
    mid = rest.astype(BF16)
    lo = (rest - mid.astype(F32)).astype(BF16)
    parts = jnp.dot(tri, jnp.concatenate([hi, mid, lo], axis=1), preferred_element_type=F32)
    return parts[:, 0:HEAD_W] + parts[:, HEAD_W:2 * HEAD_W] + parts[:, 2 * HEAD_W:3 * HEAD_W]


def _tile_phase(q, k, v_bf, b, rev):
    row = lax.broadcasted_iota(jnp.int32, (TILE, TILE), 0)
    col = lax.broadcasted_iota(jnp.int32, (TILE, TILE), 1)
    tot = b[0:1, :] if rev else b[TILE - 1:TILE, :]
    half = 0.5 * tot
    q_s = (q * jnp.exp(b - half)).astype(BF16)
    k_s = (k * jnp.exp(half - b)).astype(BF16)
    a = lax.dot_general(q_s, k_s, NT_DIMS, preferred_element_type=F32)
    a = jnp.where((col >= row) if rev else (col <= row), a, 0.0).astype(BF16)
    o_intra = jnp.dot(a, v_bf, preferred_element_type=F32)
    q_t = (q * jnp.exp(b)).astype(BF16)
    k_out = (k * jnp.exp(tot - b)).astype(BF16)
    s_loc = lax.dot_general(v_bf, k_out, TN_DIMS, preferred_element_type=F32)
    return o_intra, q_t, s_loc, jnp.exp(tot)


def _make_mixer_kernel(seqs_per_group, has_state_in, has_state_out):
    tiles_per_seq = TILES // seqs_per_group

    def kernel(*refs):
        it = iter(refs)
        x_ref = next(it); mod_ref = next(it); n1_ref = next(it); w_ref = next(it)
        wup_ref = next(it); bg_ref = next(it); lb_ref = next(it); hn_ref = next(it); wout_ref = next(it)
        s0_ref = next(it) if has_state_in else None
        x1_ref = next(it)
        st_ref = next(it) if has_state_out else None
        (h_scr, proj_scr, g_scr, k_scr, b_scr, qin_scr, kout_scr, qt_scr, vb_scr, fc_scr, o_scr,
         sloc_scr, gt_scr) = it

        hd = pl.program_id(1)

        @pl.when(hd == 0)
        def _():
            shift1 = mod_ref[0, :, 0:D_MODEL]
            scale1 = mod_ref[0, :, D_MODEL:2 * D_MODEL]
            n1 = n1_ref[...]

            def norm_rows(rows):
                h = _rms(x_ref[rows, :]) * n1
                h_scr[rows, :] = (h * (1.0 + scale1) + shift1).astype(BF16)
                x1_ref[rows, :] = jnp.zeros((ROW_BLOCK, D_MODEL), F32)
            _row_loop(GROUP_ROWS, ROW_BLOCK, norm_rows)

        def proj_rows(rows):
            proj_scr[rows, :] = jnp.dot(h_scr[rows, :], w_ref[0], preferred_element_type=F32)
        _row_loop(GROUP_ROWS, MATMUL_ROWS, proj_rows)

        @pl.when(hd < GLA_HEADS)
        def _():
            def gla_rows(rows):
                proj_scr[rows, 0:HEAD_W] = proj_scr[rows, 0:HEAD_W] * (GLA_DK ** -0.5)
                k = proj_scr[rows, HEAD_W:2 * HEAD_W]
                lr = proj_scr[rows, 2 * HEAD_W:3 * HEAD_W].astype(BF16)
                for d in range(2):
                    z = jnp.dot(lr, wup_ref[0, d], preferred_element_type=F32) + bg_ref[0, d:d + 1, :]
                    log_sig = jnp.minimum(z, 0.0) - jnp.log1p(jnp.exp(-jnp.abs(z)))
                    g_scr[d, rows, :] = log_sig * (1.0 / GLA_GATE_NORM)
                    k_scr[d, rows, :] = k
            _row_loop(GROUP_ROWS, ROW_BLOCK, gla_rows)

        @pl.when(hd >= GLA_HEADS)
        def _():
            def hgrn_rows(rows):
                for d in range(2):
                    a0 = lb_ref[0, d:d + 1, :]
                    a1 = lb_ref[0, 2 + d:3 + d, :]
                    m = jnp.maximum(a0, a1)
                    e0 = jnp.exp(a0 - m)
                    e1 = jnp.exp(a1 - m)
                    lb = e0 / (e0 + e1)
                    xr = proj_scr[rows, (1 + d) * HEAD_W:(2 + d) * HEAD_W]
                    f = lb + (1.0 - lb) / (1.0 + jnp.exp(-xr))
                    g_scr[d, rows, :] = jnp.log(f)
                    k_scr[d, rows, :] = (1.0 - lb) / (1.0 + jnp.exp(xr))
            _row_loop(GROUP_ROWS, ROW_BLOCK, hgrn_rows)

        def cum_body(t, lowest):
            rows = pl.ds(pl.multiple_of(t * TILE, TILE), TILE)
            for d in range(2):
                b = _tile_cumsum(g_scr[d, rows, :], rev=(d == 1))
                b_scr[d, rows, :] = b
                lowest = jnp.minimum(lowest, b[0:1, :] if d == 1 else b[TILE - 1:TILE, :])
            return lowest
        lowest = lax.fori_loop(0, TILES, cum_body, jnp.zeros((1, HEAD_W), F32), unroll=2)
        tile_safe = jnp.min(lowest) >= -TILE_DECAY_LIMIT

        @pl.when(tile_safe)
        def _():
            def tile_body(t, carry):
                rows = pl.ds(pl.multiple_of(t * TILE, TILE), TILE)
                q = proj_scr[rows, 0:HEAD_W]
                v_bf = proj_scr[rows, 3 * HEAD_W:4 * HEAD_W].astype(BF16)
                o_sum = None
                for d in range(2):
                    o_intra, q_t, s_loc, g_t = _tile_phase(q, k_scr[d, rows, :], v_bf, b_scr[d, rows, :], rev=(d == 1))
                    qt_scr[d, rows, :] = q_t
                    sloc_scr[d, t] = s_loc
                    gt_scr[pl.ds(d * TILES + t, 1), :] = g_t
                    o_sum = o_intra if o_sum is None else o_sum + o_intra
                o_scr[rows, :] = o_sum
                return carry
            lax.fori_loop(0, TILES, tile_body, 0, unroll=2)

        @pl.when(jnp.logical_not(tile_safe))
        def _():
            def tile_body(t, carry):
                rows = pl.ds(pl.multiple_of(t * TILE, TILE), TILE)
                q = proj_scr[rows, 0:HEAD_W]
                v_bf = proj_scr[rows, 3 * HEAD_W:4 * HEAD_W].astype(BF16)
                vb_scr[rows, :] = v_bf
                o_sum = None
                for d in range(2):
                    o_intra, q_in, k_out, fc = _chunk_phase(q, k_scr[d, rows, :], v_bf, g_scr[d, rows, :], rev=(d == 1))
                    qin_scr[d, rows, :] = q_in
                    kout_scr[d, rows, :] = k_out
                    fc_scr[d, pl.ds(pl.multiple_of(t * CHUNKS_PER_TILE, CHUNKS_PER_TILE), CHUNKS_PER_TILE), :] = fc
                    o_sum = o_intra if o_sum is None else o_sum + o_intra
                o_scr[rows, :] = o_sum
                return carry
            lax.fori_loop(0, TILES, tile_body, 0)

            sloc_scr[...] = jnp.zeros_like(sloc_scr)
            gt_scr[...] = jnp.ones_like(gt_scr)

            def chunk_body(c, carry):
                for d in range(2):
                    cc = c if d == 0 else CHUNKS_PER_TILE - 1 - c
                    for t in range(TILES):
                        rows = pl.ds(pl.multiple_of(t * TILE + cc * CHUNK, CHUNK), CHUNK)
                        q_in = qin_scr[d, rows, :]
                        s_loc = sloc_scr[d, t]
                        o_loc = lax.dot_general(q_in.astype(BF16), s_loc.astype(BF16), NT_DIMS,
                                                preferred_element_type=F32)
                        o_scr[rows, :] += o_loc
                        u_t = lax.dot_general(vb_scr[rows, :], kout_scr[d, rows, :], TN_DIMS,
                                              preferred_element_type=F32)
                        fc = fc_scr[d, pl.ds(t * CHUNKS_PER_TILE + cc, 1), :]
                        sloc_scr[d, t] = s_loc * fc + u_t
                        gt = gt_scr[d * TILES + t:d * TILES + t + 1, :]
                        qt_scr[d, rows, :] = (q_in * gt).astype(BF16)
                        gt_scr[d * TILES + t:d * TILES + t + 1, :] = gt * fc
                return carry
            lax.fori_loop(0, CHUNKS_PER_TILE, chunk_body, 0)

        for s in range(seqs_per_group):
            for d in range(2):
                if has_state_in:
                    s_run = s0_ref[s, d, 0].T
                else:
                    s_run = jnp.zeros((HEAD_W, HEAD_W), F32)
                order = range(tiles_per_seq) if d == 0 else range(tiles_per_seq - 1, -1, -1)
                for tt in order:
                    t = s * tiles_per_seq + tt
                    rows = pl.ds(t * TILE, TILE)
                    o_scr[rows, :] += lax.dot_general(qt_scr[d, rows, :], s_run.astype(BF16), NT_DIMS,
                                                      preferred_element_type=F32)
                    s_run = s_run * gt_scr[d * TILES + t:d * TILES + t + 1, :] + sloc_scr[d, t]
                if has_state_out:
                    st_ref[s, d, 0] = s_run.T

        def gate_rows(rows):
            o = _rms(o_scr[rows, :]) * hn_ref[0]
            vb_scr[rows, :] = (o * _silu(proj_scr[rows, 4 * HEAD_W:5 * HEAD_W])).astype(BF16)
        _row_loop(GROUP_ROWS, ROW_BLOCK, gate_rows)

        def out_rows(rows):
            x1_ref[rows, :] += jnp.dot(vb_scr[rows, :], wout_ref[0], preferred_element_type=F32)
        _row_loop(GROUP_ROWS, MATMUL_ROWS, out_rows)

        @pl.when(hd == N_HEADS - 1)
        def _():
            gate1 = mod_ref[0, :, 2 * D_MODEL:3 * D_MODEL]

            def res_rows(rows):
                x1_ref[rows, :] = x_ref[rows, :] + gate1 * x1_ref[rows, :]
            _row_loop(GROUP_ROWS, ROW_BLOCK, res_rows)

    return kernel


def _mixer(x, mod3, mod_row, norm1, w_heads, wup, bgl, lbl, hnorm, wout_h, s0, seqs_per_group, want_states):
    groups = x.shape[0] // GROUP_ROWS
    n_seq = groups * seqs_per_group
    has_state_in = s0 is not None
    once = pl.Buffered(1)
    in_specs = [
        pl.BlockSpec((GROUP_ROWS, D_MODEL), lambda g, h: (g, 0), pipeline_mode=once),
        pl.BlockSpec((1, 1, 6 * D_MODEL), lambda g, h: (mod_row(g), 0, 0)),
        pl.BlockSpec((1, D_MODEL), lambda g, h: (0, 0)),
        pl.BlockSpec((1, D_MODEL, PROJ_W), lambda g, h: (h, 0, 0)),
        pl.BlockSpec((1, 2, HEAD_W, HEAD_W), lambda g, h: (h, 0, 0, 0)),
        pl.BlockSpec((1, 2, HEAD_W), lambda g, h: (h, 0, 0)),
        pl.BlockSpec((1, 4, HEAD_W), lambda g, h: (h, 0, 0)),
        pl.BlockSpec((1, 1, HEAD_W), lambda g, h: (h, 0, 0)),
        pl.BlockSpec((1, HEAD_W, D_MODEL), lambda g, h: (h, 0, 0)),
    ]
    args = [x, mod3, norm1, w_heads, wup, bgl, lbl, hnorm, wout_h]
    if has_state_in:
        in_specs.append(pl.BlockSpec((seqs_per_group, 2, 1, HEAD_W, HEAD_W), lambda g, h: (g, 0, h, 0, 0)))
        args.append(s0)
    out_specs = [pl.BlockSpec((GROUP_ROWS, D_MODEL), lambda g, h: (g, 0), pipeline_mode=once)]
    out_shape = [jax.ShapeDtypeStruct(x.shape, F32)]
    if want_states:
        out_specs.append(pl.BlockSpec((seqs_per_group, 2, 1, HEAD_W, HEAD_W), lambda g, h: (g, 0, h, 0, 0)))
        out_shape.append(jax.ShapeDtypeStruct((n_seq, 2, N_HEADS, HEAD_W, HEAD_W), F32))
    scratch = [
        pltpu.VMEM((GROUP_ROWS, D_MODEL), BF16),
        pltpu.VMEM((GROUP_ROWS, PROJ_W), F32),
        pltpu.VMEM((2, GROUP_ROWS, HEAD_W), F32),
        pltpu.VMEM((2, GROUP_ROWS, HEAD_W), F32),
        pltpu.VMEM((2, GROUP_ROWS, HEAD_W), F32),
        pltpu.VMEM((2, GROUP_ROWS, HEAD_W), F32),
        pltpu.VMEM((2, GROUP_ROWS, HEAD_W), BF16),
        pltpu.VMEM((2, GROUP_ROWS, HEAD_W), BF16),
        pltpu.VMEM((GROUP_ROWS, HEAD_W), BF16),
        pltpu.VMEM((2, GROUP_ROWS // CHUNK, HEAD_W), F32),
        pltpu.VMEM((GROUP_ROWS, HEAD_W), F32),
        pltpu.VMEM((2, TILES, HEAD_W, HEAD_W), F32),
        pltpu.VMEM((2 * TILES, HEAD_W), F32),
    ]
    outs = pl.pallas_call(
        _make_mixer_kernel(seqs_per_group, has_state_in, want_states),
        grid=(groups, N_HEADS),
        in_specs=in_specs,
        out_specs=out_specs,
        out_shape=out_shape,
        scratch_shapes=scratch,
        compiler_params=pltpu.CompilerParams(dimension_semantics=("arbitrary", "arbitrary"),
                                             vmem_limit_bytes=VMEM_LIMIT),
        name="mixer_ctx" if want_states else "mixer_lat",
    )(*args)
    return outs


def _make_ffn_kernel(grid_conv, seq_len):
    width = GRID_W if grid_conv else seq_len
    dys = (-1, 0, 1) if grid_conv else (0,)
    n_blocks = GROUP_ROWS // FFN_ROWS
    halo = 8
    win = ROW_BLOCK + 2 * halo

    def kernel(x1_ref, mod_ref, n2_ref, wg0_ref, wu0_ref, wgn_ref, wun_ref, cwg_ref, cwu_ref, cbg_ref, cbu_ref,
               wd_ref, fn_ref, y_ref, h_scr, pg_scr, pu_scr, act_scr):
        j = pl.program_id(1)
        slot = j & 1

        def up_block(i, wg_ref, wu_ref, dst_slot):
            start = _block_start(i, FFN_ROWS)
            h = h_scr[pl.ds(start, FFN_ROWS), :]
            dst = pl.ds(_aligned(start + CONV_PAD, 8), FFN_ROWS)
            pg_scr[dst_slot, dst, :] = jnp.dot(h, wg_ref[...], preferred_element_type=F32)
            pu_scr[dst_slot, dst, :] = jnp.dot(h, wu_ref[...], preferred_element_type=F32)

        @pl.when(j == 0)
        def _():
            shift2 = mod_ref[0, :, 3 * D_MODEL:4 * D_MODEL]
            scale2 = mod_ref[0, :, 4 * D_MODEL:5 * D_MODEL]
            n2 = n2_ref[...]

            def norm_rows(rows):
                h = _rms(x1_ref[rows, :]) * n2
                h_scr[rows, :] = (h * (1.0 + scale2) + shift2).astype(BF16)
                y_ref[rows, :] = jnp.zeros((ROW_BLOCK, D_MODEL), F32)
            _row_loop(GROUP_ROWS, ROW_BLOCK, norm_rows)
            zeros = jnp.zeros((CONV_PAD, FFN_TILE), F32)
            for scr in (pg_scr, pu_scr):
                for s in range(2):
                    scr[s, 0:CONV_PAD, :] = zeros
                    scr[s, CONV_PAD + GROUP_ROWS:2 * CONV_PAD + GROUP_ROWS, :] = zeros

            def first_up(i, carry):
                up_block(i, wg0_ref, wu0_ref, 0)
                return carry
            lax.fori_loop(0, n_blocks, first_up, 0)

        def conv(scr, cw_ref, cb_ref, start, pos):
            wins = [scr[slot, pl.ds(_aligned(start + CONV_PAD + dy * GRID_W - halo, 8), win), :] for dy in dys]
            acc = None
            for dx in (-1, 0, 1):
                part = None
                for dy, w in zip(dys, wins):
                    tap = cw_ref[(dy + 1) * 3 + (dx + 1):(dy + 1) * 3 + (dx + 2), :]
                    part = w * tap if part is None else part + w * tap
                if dx:
                    part = pltpu.roll(part, 1 if dx == -1 else win - 1, 0)
                part = part[halo:halo + ROW_BLOCK]
                if dx == -1:
                    part = jnp.where(pos != 0, part, 0.0)
                elif dx == 1:
                    part = jnp.where(pos != width - 1, part, 0.0)
                acc = part if acc is None else acc + part
            return acc + cb_ref[...]

        def down_block(i):
            start = _block_start(i, FFN_ROWS)
            for part in range(FFN_ROWS // ROW_BLOCK):
                sub = start + part * ROW_BLOCK
                ridx = sub + lax.broadcasted_iota(jnp.int32, (ROW_BLOCK, FFN_TILE), 0)
                pos = ridx & (width - 1)
                cg = conv(pg_scr, cwg_ref, cbg_ref, sub, pos)
                cu = conv(pu_scr, cwu_ref, cbu_ref, sub, pos)
                act_scr[part * ROW_BLOCK:(part + 1) * ROW_BLOCK, :] = (_silu(cg) * cu).astype(BF16)
            y_ref[pl.ds(start, FFN_ROWS), :] += jnp.dot(act_scr[...], wd_ref[...], preferred_element_type=F32)

        @pl.when(j < FFN_STEPS - 1)
        def _():
            def step(i, carry):
                down_block(i)
                up_block(i, wgn_ref, wun_ref, 1 - slot)
                return carry
            lax.fori_loop(0, n_blocks, step, 0)

        @pl.when(j == FFN_STEPS - 1)
        def _():
            def step(i, carry):
                down_block(i)
                return carry
            lax.fori_loop(0, n_blocks, step, 0)
            gate2 = mod_ref[0, :, 5 * D_MODEL:6 * D_MODEL]
            fn = fn_ref[...]

            def res_rows(rows):
                y_ref[rows, :] = _rms(x1_ref[rows, :] + gate2 * y_ref[rows, :]) * fn
            _row_loop(GROUP_ROWS, ROW_BLOCK, res_rows)

    return kernel


def _conv_ffn(x1, mod3, mod_row, norm2, w_up, conv_w, conv_b, w_down, final_norm, grid_conv, seq_len):
    groups = x1.shape[0] // GROUP_ROWS
    once = pl.Buffered(1)
    nxt = lambda j: jnp.minimum(j + 1, FFN_STEPS - 1)
    in_specs = [
        pl.BlockSpec((GROUP_ROWS, D_MODEL), lambda g, j: (g, 0), pipeline_mode=once),
        pl.BlockSpec((1, 1, 6 * D_MODEL), lambda g, j: (mod_row(g), 0, 0)),
        pl.BlockSpec((1, D_MODEL), lambda g, j: (0, 0)),
        pl.BlockSpec((D_MODEL, FFN_TILE), lambda g, j: (0, 0)),
        pl.BlockSpec((D_MODEL, FFN_TILE), lambda g, j: (0, FFN_STEPS)),
        pl.BlockSpec((D_MODEL, FFN_TILE), lambda g, j: (0, nxt(j))),
        pl.BlockSpec((D_MODEL, FFN_TILE), lambda g, j: (0, FFN_STEPS + nxt(j))),
        pl.BlockSpec((9, FFN_TILE), lambda g, j: (0, j)),
        pl.BlockSpec((9, FFN_TILE), lambda g, j: (0, FFN_STEPS + j)),
        pl.BlockSpec((1, FFN_TILE), lambda g, j: (0, j)),
        pl.BlockSpec((1, FFN_TILE), lambda g, j: (0, FFN_STEPS + j)),
        pl.BlockSpec((FFN_TILE, D_MODEL), lambda g, j: (j, 0)),
        pl.BlockSpec((1, D_MODEL), lambda g, j: (0, 0)),
    ]
    scratch = [
        pltpu.VMEM((GROUP_ROWS, D_MODEL), BF16),
        pltpu.VMEM((2, GROUP_ROWS + 2 * CONV_PAD, FFN_TILE), F32),
        pltpu.VMEM((2, GROUP_ROWS + 2 * CONV_PAD, FFN_TILE), F32),
        pltpu.VMEM((FFN_ROWS, FFN_TILE), BF16),
    ]
    return pl.pallas_call(
        _make_ffn_kernel(grid_conv, seq_len),
        grid=(groups, FFN_STEPS),
        in_specs=in_specs,
        out_specs=pl.BlockSpec((GROUP_ROWS, D_MODEL), lambda g, j: (g, 0), pipeline_mode=once),
        out_shape=jax.ShapeDtypeStruct(x1.shape, F32),
        scratch_shapes=scratch,
        compiler_params=pltpu.CompilerParams(dimension_semantics=("arbitrary", "arbitrary"),
                                             vmem_limit_bytes=VMEM_LIMIT),
        name="conv_ffn_lat" if grid_conv else "conv_ffn_ctx",
    )(x1, mod3, norm2, w_up, w_up, w_up, w_up, conv_w, conv_w, conv_b, conv_b, w_down, final_norm)


def _head_layout(w_in, w_gla_up, b_gla, hgrn_lb, gla_norm, hgrn_norm, w_out):
    gla_kw = GLA_HEADS * GLA_DK
    vw = GLA_HEADS * HEAD_W
    o_qa, o_ka, o_va, o_ga = 0, gla_kw, 2 * gla_kw, 2 * gla_kw + vw
    o_lr = o_ga + vw
    o_qb = o_lr + 2 * GLA_LOWRANK
    o_fb = o_qb + vw
    o_ib = o_fb + 2 * vw
    o_gb = o_ib + vw
    wb = w_in.astype(BF16)
    zeros = lambda n: jnp.zeros((D_MODEL, n), BF16)
    heads = []
    for h in range(GLA_HEADS):
        heads.append(jnp.concatenate([
            wb[:, o_qa + h * GLA_DK:o_qa + (h + 1) * GLA_DK], zeros(HEAD_W - GLA_DK),
            wb[:, o_ka + h * GLA_DK:o_ka + (h + 1) * GLA_DK], zeros(HEAD_W - GLA_DK),
            wb[:, o_lr:o_lr + 2 * GLA_LOWRANK], zeros(HEAD_W - 2 * GLA_LOWRANK),
            wb[:, o_va + h * HEAD_W:o_va + (h + 1) * HEAD_W],
            wb[:, o_ga + h * HEAD_W:o_ga + (h + 1) * HEAD_W]], axis=1))
    for h in range(N_HEADS - GLA_HEADS):
        sl = slice(h * HEAD_W, (h + 1) * HEAD_W)
        heads.append(jnp.concatenate([
            wb[:, o_qb:o_qb + vw][:, sl], wb[:, o_fb:o_fb + vw][:, sl], wb[:, o_fb + vw:o_fb + 2 * vw][:, sl],
            wb[:, o_ib:o_ib + vw][:, sl], wb[:, o_gb:o_gb + vw][:, sl]], axis=1))
    w_heads = jnp.stack(heads, axis=0)

    wup = jnp.zeros((N_HEADS, 2, HEAD_W, HEAD_W), BF16)
    bgl = jnp.zeros((N_HEADS, 2, HEAD_W), F32)
    for h in range(GLA_HEADS):
        for d in range(2):
            wup = wup.at[h, d, d * GLA_LOWRANK:(d + 1) * GLA_LOWRANK, 0:GLA_DK].set(
                w_gla_up[d, :, h * GLA_DK:(h + 1) * GLA_DK].astype(BF16))
            bgl = bgl.at[h, d, 0:GLA_DK].set(b_gla[d, h * GLA_DK:(h + 1) * GLA_DK])
    lb = hgrn_lb.astype(F32).reshape(2, 2, N_HEADS - GLA_HEADS, HEAD_W)
    lbl = jnp.concatenate([jnp.zeros((GLA_HEADS, 4, HEAD_W), F32),
                           lb.transpose(2, 0, 1, 3).reshape(N_HEADS - GLA_HEADS, 4, HEAD_W)], axis=0)
    hnorm = jnp.concatenate([jnp.broadcast_to(gla_norm[None, None, :], (GLA_HEADS, 1, HEAD_W)),
                             jnp.broadcast_to(hgrn_norm[None, None, :], (N_HEADS - GLA_HEADS, 1, HEAD_W))], axis=0)
    wout_h = w_out.astype(BF16).reshape(N_HEADS, HEAD_W, D_MODEL)
    return w_heads, wup, bgl, lbl, hnorm, wout_h


def kernel(x_prompt, x_sample, state_gla, state_hgrn, c, c_ctx, w_ada, b_ada, norm1, norm2, w_in, w_gla_up, b_gla, hgrn_lb, gla_norm, hgrn_norm, w_out, w_ffn_up, ffn_conv, b_ffn_conv, w_ffn_down, final_norm):
    assert w_ada.shape[0] == 1 and hgrn_lb.shape[0] == 2, "single layer only"
    n_ctx, ctx_len, _ = x_prompt.shape
    n_lat, lat_len, _ = x_sample.shape
    assert lat_len == GROUP_ROWS and GROUP_ROWS % ctx_len == 0 and n_ctx % (GROUP_ROWS // ctx_len) == 0
    assert n_lat + 1 <= 8
    ctx_per_group = GROUP_ROWS // ctx_len

    cvecs = jnp.concatenate([c_ctx[None, :], c, jnp.zeros((8 - 1 - n_lat, D_MODEL), F32)], axis=0)
    mod3 = _modulation(cvecs, w_ada[0], b_ada).reshape(8, 1, 6 * D_MODEL)

    w_heads, wup, bgl, lbl, hnorm, wout_h = _head_layout(
        w_in[0], w_gla_up[0], b_gla[0], hgrn_lb, gla_norm[0], hgrn_norm[0], w_out[0])
    s0 = jnp.concatenate([jnp.pad(state_gla[:, 0], ((0, 0), (0, 0), (0, 0), (0, HEAD_W - GLA_DK), (0, 0))),
                          state_hgrn[:, 0]], axis=2)
    w_up = w_ffn_up[0].astype(BF16)
    w_down = w_ffn_down[0].astype(BF16)
    conv_w = ffn_conv[0].reshape(9, 2 * FFN_HIDDEN)
    conv_b = b_ffn_conv
    ctx_row = lambda g: 0
    lat_row = lambda g: g + 1

    xp = x_prompt.reshape(n_ctx * ctx_len, D_MODEL)
    xs = x_sample.reshape(n_lat * lat_len, D_MODEL)
    xp1, states = _mixer(xp, mod3, ctx_row, norm1, w_heads, wup, bgl, lbl, hnorm, wout_h, None,
                         ctx_per_group, True)
    (xs1,) = _mixer(xs, mod3, lat_row, norm1, w_heads, wup, bgl, lbl, hnorm, wout_h, s0, 1, False)
    yp = _conv_ffn(xp1, mod3, ctx_row, norm2, w_up, conv_w, conv_b, w_down, final_norm[None, :], False, ctx_len)
    ys = _conv_ffn(xs1, mod3, lat_row, norm2, w_up, conv_w, conv_b, w_down, final_norm[None, :], True, lat_len)

    new_gla = states[:, None, :, :GLA_HEADS, :GLA_DK, :]
    new_hgrn = states[:, None, :, GLA_HEADS:, :, :]
    return (yp.reshape(x_prompt.shape), ys.reshape(x_sample.shape), new_gla, new_hgrn)
```

```python
import jax
import jax.numpy as jnp
from jax import lax
from jax.experimental import pallas as pl
from jax.experimental.pallas import tpu as pltpu

F32 = jnp.float32
BF16 = jnp.bfloat16

D_MODEL = 1024
N_HEADS = 8
GLA_HEADS = 4
GLA_DK = 64
HEAD_W = 128
GLA_LOWRANK = 16
GLA_GATE_NORM = 16.0
FFN_HIDDEN = 2816
GRID_W = 64
EPS = 1e-6

GROUP_ROWS = 2048
TILE = 128
CHUNK = 16
CHUNKS_PER_TILE = TILE // CHUNK
TILE_DECAY_LIMIT = 150.0
TILES = GROUP_ROWS // TILE
ROW_BLOCK = 256
MATMUL_ROWS = 1024
PROJ_W = 5 * HEAD_W
FFN_TILE = 256
FFN_STEPS = FFN_HIDDEN // FFN_TILE
CONV_PAD = 72
VMEM_LIMIT = 56 * 1024 * 1024

NT_DIMS = (((1,), (1,)), ((), ()))
TN_DIMS = (((0,), (0,)), ((), ()))


def _aligned(x, m):
    return x if isinstance(x, int) else pl.multiple_of(x, m)


def _block_start(i, block):
    return _aligned(i * block, block)


def _row_loop(n_rows, block, body, unroll=1):
    def step(i, carry):
        body(pl.ds(pl.multiple_of(i * block, block), block))
        return carry
    lax.fori_loop(0, n_rows // block, step, 0, unroll=unroll)


def _rms(x):
    return x * lax.rsqrt(jnp.mean(x * x, axis=-1, keepdims=True) + EPS)


def _silu(x):
    return x / (1.0 + jnp.exp(-x))


def _mod_kernel(c_ref, w_ref, b_ref, o_ref):
    a = _silu(c_ref[...]).astype(BF16)
    o_ref[...] = jnp.dot(a, w_ref[...].astype(BF16), preferred_element_type=F32) + b_ref[...]


def _modulation(cvecs, w_ada, b_ada):
    n = w_ada.shape[1]
    tn = 512
    return pl.pallas_call(
        _mod_kernel,
        grid=(n // tn,),
        in_specs=[pl.BlockSpec((8, D_MODEL), lambda j: (0, 0)),
                  pl.BlockSpec((D_MODEL, tn), lambda j: (0, j)),
                  pl.BlockSpec((1, tn), lambda j: (0, j))],
        out_specs=pl.BlockSpec((8, tn), lambda j: (0, j)),
        out_shape=jax.ShapeDtypeStruct((8, n), F32),
        name="modulation",
    )(cvecs, w_ada, b_ada)


def _chunk_phase(q, k, v_bf, g, rev):
    row = lax.broadcasted_iota(jnp.int32, (TILE, HEAD_W), 0)
    col = lax.broadcasted_iota(jnp.int32, (TILE, HEAD_W), 1)
    cpos = row & (CHUNK - 1)
    f = jnp.exp(g)
    if rev:
        fz = jnp.where(cpos == CHUNK - 1, 0.0, f)
        diff = row - col
        edge_row = 0
    else:
        fz = jnp.where(cpos == 0, 0.0, f)
        diff = col - row
        edge_row = CHUNK - 1
    b = g
    for s in (1, 2, 4, 8):
        if rev:
            b = b + jnp.where(cpos <= CHUNK - 1 - s, pltpu.roll(b, TILE - s, 0), 0.0)
        else:
            b = b + jnp.where(cpos >= s, pltpu.roll(b, s, 0), 0.0)
    tot_rows = [b[j * CHUNK + edge_row:j * CHUNK + edge_row + 1, :] for j in range(CHUNKS_PER_TILE)]
    b_tot = jnp.concatenate([jnp.broadcast_to(r, (CHUNK, HEAD_W)) for r in tot_rows], axis=0)
    fc = jnp.exp(jnp.concatenate(tot_rows, axis=0))
    q_in = q * jnp.exp(b)
    k_out = (k * jnp.exp(b_tot - b)).astype(BF16)
    w = k
    a = jnp.zeros((TILE, TILE), F32)
    shift = TILE - 1 if rev else 1
    for d in range(CHUNK):
        if d:
            w = pltpu.roll(w, shift, 0) * fz
        s = jnp.sum(q * w, axis=1, keepdims=True)
        a = jnp.where(diff == -d, s, a)
    o_intra = jnp.dot(a.astype(BF16), v_bf, preferred_element_type=F32)
    return o_intra, q_in, k_out, fc


def _tile_cumsum(g, rev):
    row = lax.broadcasted_iota(jnp.int32, (TILE, TILE), 0)
    col = lax.broadcasted_iota(jnp.int32, (TILE, TILE), 1)
    tri = jnp.where((col >= row) if rev else (col <= row), 1.0, 0.0).astype(BF16)
    hi = g.astype(BF16)
    rest = g - hi.astype(F32)
    mid = rest.astype(BF16)
    lo = (rest - mid.astype(F32)).astype(BF16)
    parts = jnp.dot(tri, jnp.concatenate([hi, mid, lo], axis=1), preferred_element_type=F32)
    return parts[:, 0:HEAD_W] + parts[:, HEAD_W:2 * HEAD_W] + parts[:, 2 * HEAD_W:3 * HEAD_W]


def _tile_phase(q, k, v_bf, b, rev):
    row = lax.broadcasted_iota(jnp.int32, (TILE, TILE), 0)
    col = lax.broadcasted_iota(jnp.int32, (TILE, TILE), 1)
    tot = b[0:1, :] if rev else b[TILE - 1:TILE, :]
    half = 0.5 * tot
    q_s = (q * jnp.exp(b - half)).astype(BF16)
    k_s = (k * jnp.exp(half - b)).astype(BF16)
    a = lax.dot_general(q_s, k_s, NT_DIMS, preferred_element_type=F32)
    a = jnp.where((col >= row) if rev else (col <= row), a, 0.0).astype(BF16)
    o_intra = jnp.dot(a, v_bf, preferred_element_type=F32)
    q_t = (q * jnp.exp(b)).astype(BF16)
    k_out = (k * jnp.exp(tot - b)).astype(BF16)
    s_loc = lax.dot_general(v_bf, k_out, TN_DIMS, preferred_element_type=F32)
    return o_intra, q_t, s_loc, jnp.exp(tot)


def _make_mixer_kernel(seqs_per_group, has_state_in, has_state_out):
    tiles_per_seq = TILES // seqs_per_group

    def kernel(*refs):
        it = iter(refs)
        x_ref = next(it); mod_ref = next(it); n1_ref = next(it); w_ref = next(it)
        wup_ref = next(it); bg_ref = next(it); lb_ref = next(it); hn_ref = next(it); wout_ref = next(it)
        s0_ref = next(it) if has_state_in else None
        x1_ref = next(it)
        sg_ref = next(it) if has_state_out else None
        sh_ref = next(it) if has_state_out else None
        (h_scr, proj_scr, g_scr, k_scr, b_scr, qin_scr, kout_scr, qt_scr, vb_scr, fc_scr, o_scr,
         sloc_scr, gt_scr, st_scr) = it

        hd = pl.program_id(1)

        @pl.when(hd == 0)
        def _():
            shift1 = mod_ref[0, :, 0:D_MODEL]
            scale1 = mod_ref[0, :, D_MODEL:2 * D_MODEL]
            n1 = n1_ref[...]

            def norm_rows(rows):
                h = _rms(x_ref[rows, :]) * n1
                h_scr[rows, :] = (h * (1.0 + scale1) + shift1).astype(BF16)
                x1_ref[rows, :] = jnp.zeros((ROW_BLOCK, D_MODEL), F32)
            _row_loop(GROUP_ROWS, ROW_BLOCK, norm_rows)

        def proj_rows(rows):
            proj_scr[rows, :] = jnp.dot(h_scr[rows, :], w_ref[0], preferred_element_type=F32)
        _row_loop(GROUP_ROWS, MATMUL_ROWS, proj_rows)

        @pl.when(hd < GLA_HEADS)
        def _():
            def gla_rows(rows):
                proj_scr[rows, 0:HEAD_W] = proj_scr[rows, 0:HEAD_W] * (GLA_DK ** -0.5)
                k = proj_scr[rows, HEAD_W:2 * HEAD_W]
                lr = proj_scr[rows, 2 * HEAD_W:3 * HEAD_W].astype(BF16)
                for d in range(2):
                    z = jnp.dot(lr, wup_ref[0, d], preferred_element_type=F32) + bg_ref[0, d:d + 1, :]
                    log_sig = jnp.minimum(z, 0.0) - jnp.log1p(jnp.exp(-jnp.abs(z)))
                    g_scr[d, rows, :] = log_sig * (1.0 / GLA_GATE_NORM)
                    k_scr[d, rows, :] = k
            _row_loop(GROUP_ROWS, ROW_BLOCK, gla_rows, unroll=2)

        @pl.when(hd >= GLA_HEADS)
        def _():
            def hgrn_rows(rows):
                for d in range(2):
                    a0 = lb_ref[0, d:d + 1, :]
                    a1 = lb_ref[0, 2 + d:3 + d, :]
                    m = jnp.maximum(a0, a1)
                    e0 = jnp.exp(a0 - m)
                    e1 = jnp.exp(a1 - m)
                    lb = e0 / (e0 + e1)
                    xr = proj_scr[rows, (1 + d) * HEAD_W:(2 + d) * HEAD_W]
                    t = jnp.exp(-jnp.abs(xr))
                    big = 1.0 / (1.0 + t)
                    small = t * big
                    pos = xr >= 0.0
                    g_scr[d, rows, :] = jnp.log(lb + (1.0 - lb) * jnp.where(pos, big, small))
                    k_scr[d, rows, :] = (1.0 - lb) * jnp.where(pos, small, big)
            _row_loop(GROUP_ROWS, ROW_BLOCK, hgrn_rows, unroll=2)

        def cum_body(t, lowest):
            rows = pl.ds(pl.multiple_of(t * TILE, TILE), TILE)
            for d in range(2):
                b = _tile_cumsum(g_scr[d, rows, :], rev=(d == 1))
                b_scr[d, rows, :] = b
                lowest = jnp.minimum(lowest, b[0:1, :] if d == 1 else b[TILE - 1:TILE, :])
            return lowest
        lowest = lax.fori_loop(0, TILES, cum_body, jnp.zeros((1, HEAD_W), F32), unroll=TILES)
        tile_safe = jnp.min(lowest) >= -TILE_DECAY_LIMIT

        @pl.when(tile_safe)
        def _():
            def tile_body(t, carry):
                rows = pl.ds(pl.multiple_of(t * TILE, TILE), TILE)
                q = proj_scr[rows, 0:HEAD_W]
                v_bf = proj_scr[rows, 3 * HEAD_W:4 * HEAD_W].astype(BF16)
                o_sum = None
                for d in range(2):
                    o_intra, q_t, s_loc, g_t = _tile_phase(q, k_scr[d, rows, :], v_bf, b_scr[d, rows, :], rev=(d == 1))
                    qt_scr[d, rows, :] = q_t
                    sloc_scr[d, t] = s_loc
                    gt_scr[pl.ds(d * TILES + t, 1), :] = g_t
                    o_sum = o_intra if o_sum is None else o_sum + o_intra
                o_scr[rows, :] = o_sum
                return carry
            lax.fori_loop(0, TILES, tile_body, 0, unroll=TILES)

        @pl.when(jnp.logical_not(tile_safe))
        def _():
            def tile_body(t, carry):
                rows = pl.ds(pl.multiple_of(t * TILE, TILE), TILE)
                q = proj_scr[rows, 0:HEAD_W]
                v_bf = proj_scr[rows, 3 * HEAD_W:4 * HEAD_W].astype(BF16)
                vb_scr[rows, :] = v_bf
                o_sum = None
                for d in range(2):
                    o_intra, q_in, k_out, fc = _chunk_phase(q, k_scr[d, rows, :], v_bf, g_scr[d, rows, :], rev=(d == 1))
                    qin_scr[d, rows, :] = q_in
                    kout_scr[d, rows, :] = k_out
                    fc_scr[d, pl.ds(pl.multiple_of(t * CHUNKS_PER_TILE, CHUNKS_PER_TILE), CHUNKS_PER_TILE), :] = fc
                    o_sum = o_intra if o_sum is None else o_sum + o_intra
                o_scr[rows, :] = o_sum
                return carry
            lax.fori_loop(0, TILES, tile_body, 0)

            sloc_scr[...] = jnp.zeros_like(sloc_scr)
            gt_scr[...] = jnp.ones_like(gt_scr)

            def chunk_body(c, carry):
                for d in range(2):
                    cc = c if d == 0 else CHUNKS_PER_TILE - 1 - c
                    for t in range(TILES):
                        rows = pl.ds(pl.multiple_of(t * TILE + cc * CHUNK, CHUNK), CHUNK)
                        q_in = qin_scr[d, rows, :]
                        s_loc = sloc_scr[d, t]
                        o_loc = lax.dot_general(q_in.astype(BF16), s_loc.astype(BF16), NT_DIMS,
                                                preferred_element_type=F32)
                        o_scr[rows, :] += o_loc
                        u_t = lax.dot_general(vb_scr[rows, :], kout_scr[d, rows, :], TN_DIMS,
                                              preferred_element_type=F32)
                        fc = fc_scr[d, pl.ds(t * CHUNKS_PER_TILE + cc, 1), :]
                        sloc_scr[d, t] = s_loc * fc + u_t
                        gt = gt_scr[d * TILES + t:d * TILES + t + 1, :]
                        qt_scr[d, rows, :] = (q_in * gt).astype(BF16)
                        gt_scr[d * TILES + t:d * TILES + t + 1, :] = gt * fc
                return carry
            lax.fori_loop(0, CHUNKS_PER_TILE, chunk_body, 0)

        for s in range(seqs_per_group):
            for d in range(2):
                if has_state_in:
                    s_run = s0_ref[s, d, 0].T
                else:
                    s_run = jnp.zeros((HEAD_W, HEAD_W), F32)
                order = range(tiles_per_seq) if d == 0 else range(tiles_per_seq - 1, -1, -1)
                for tt in order:
                    t = s * tiles_per_seq + tt
                    rows = pl.ds(t * TILE, TILE)
                    o_scr[rows, :] += lax.dot_general(qt_scr[d, rows, :], s_run.astype(BF16), NT_DIMS,
                                                      preferred_element_type=F32)
                    s_run = s_run * gt_scr[d * TILES + t:d * TILES + t + 1, :] + sloc_scr[d, t]
                if has_state_out:
                    st_scr[s, d] = s_run.T

        if has_state_out:
            @pl.when(hd < GLA_HEADS)
            def _():
                sg_ref[:, 0, :, 0] = st_scr[:, :, 0:GLA_DK, :]

            @pl.when(hd >= GLA_HEADS)
            def _():
                sh_ref[:, 0, :, 0] = st_scr[...]

        def gate_rows(rows):
            o = _rms(o_scr[rows, :]) * hn_ref[0]
            vb_scr[rows, :] = (o * _silu(proj_scr[rows, 4 * HEAD_W:5 * HEAD_W])).astype(BF16)
        _row_loop(GROUP_ROWS, ROW_BLOCK, gate_rows, unroll=2)

        def out_rows(rows):
            x1_ref[rows, :] += jnp.dot(vb_scr[rows, :], wout_ref[0], preferred_element_type=F32)
        _row_loop(GROUP_ROWS, MATMUL_ROWS, out_rows)

        @pl.when(hd == N_HEADS - 1)
        def _():
            gate1 = mod_ref[0, :, 2 * D_MODEL:3 * D_MODEL]

            def res_rows(rows):
                x1_ref[rows, :] = x_ref[rows, :] + gate1 * x1_ref[rows, :]
            _row_loop(GROUP_ROWS, ROW_BLOCK, res_rows)

    return kernel


def _mixer(x, mod3, mod_row, norm1, w_heads, wup, bgl, lbl, hnorm, wout_h, s0, seqs_per_group, want_states):
    groups = x.shape[0] // GROUP_ROWS
    n_seq = groups * seqs_per_group
    has_state_in = s0 is not None
    once = pl.Buffered(1)
    in_specs = [
        pl.BlockSpec((GROUP_ROWS, D_MODEL), lambda g, h: (g, 0), pipeline_mode=once),
        pl.BlockSpec((1, 1, 6 * D_MODEL), lambda g, h: (mod_row(g), 0, 0)),
        pl.BlockSpec((1, D_MODEL), lambda g, h: (0, 0)),
        pl.BlockSpec((1, D_MODEL, PROJ_W), lambda g, h: (h, 0, 0)),
        pl.BlockSpec((1, 2, HEAD_W, HEAD_W), lambda g, h: (h, 0, 0, 0)),
        pl.BlockSpec((1, 2, HEAD_W), lambda g, h: (h, 0, 0)),
        pl.BlockSpec((1, 4, HEAD_W), lambda g, h: (h, 0, 0)),
        pl.BlockSpec((1, 1, HEAD_W), lambda g, h: (h, 0, 0)),
        pl.BlockSpec((1, HEAD_W, D_MODEL), lambda g, h: (h, 0, 0)),
    ]
    args = [x, mod3, norm1, w_heads, wup, bgl, lbl, hnorm, wout_h]
    if has_state_in:
        in_specs.append(pl.BlockSpec((seqs_per_group, 2, 1, HEAD_W, HEAD_W), lambda g, h: (g, 0, h, 0, 0)))
        args.append(s0)
    out_specs = [pl.BlockSpec((GROUP_ROWS, D_MODEL), lambda g, h: (g, 0), pipeline_mode=once)]
    out_shape = [jax.ShapeDtypeStruct(x.shape, F32)]
    if want_states:
        out_specs.append(pl.BlockSpec((seqs_per_group, 1, 2, 1, GLA_DK, HEAD_W),
                                      lambda g, h: (g, 0, 0, jnp.minimum(h, GLA_HEADS - 1), 0, 0)))
        out_specs.append(pl.BlockSpec((seqs_per_group, 1, 2, 1, HEAD_W, HEAD_W),
                                      lambda g, h: (g, 0, 0, jnp.maximum(h - GLA_HEADS, 0), 0, 0)))
        out_shape.append(jax.ShapeDtypeStruct((n_seq, 1, 2, GLA_HEADS, GLA_DK, HEAD_W), F32))
        out_shape.append(jax.ShapeDtypeStruct((n_seq, 1, 2, N_HEADS - GLA_HEADS, HEAD_W, HEAD_W), F32))
    scratch = [
        pltpu.VMEM((GROUP_ROWS, D_MODEL), BF16),
        pltpu.VMEM((GROUP_ROWS, PROJ_W), F32),
        pltpu.VMEM((2, GROUP_ROWS, HEAD_W), F32),
        pltpu.VMEM((2, GROUP_ROWS, HEAD_W), F32),
        pltpu.VMEM((2, GROUP_ROWS, HEAD_W), F32),
        pltpu.VMEM((2, GROUP_ROWS, HEAD_W), F32),
        pltpu.VMEM((2, GROUP_ROWS, HEAD_W), BF16),
        pltpu.VMEM((2, GROUP_ROWS, HEAD_W), BF16),
        pltpu.VMEM((GROUP_ROWS, HEAD_W), BF16),
        pltpu.VMEM((2, GROUP_ROWS // CHUNK, HEAD_W), F32),
        pltpu.VMEM((GROUP_ROWS, HEAD_W), F32),
        pltpu.VMEM((2, TILES, HEAD_W, HEAD_W), F32),
        pltpu.VMEM((2 * TILES, HEAD_W), F32),
        pltpu.VMEM((seqs_per_group, 2, HEAD_W, HEAD_W), F32),
    ]
    outs = pl.pallas_call(
        _make_mixer_kernel(seqs_per_group, has_state_in, want_states),
        grid=(groups, N_HEADS),
        in_specs=in_specs,
        out_specs=out_specs,
        out_shape=out_shape,
        scratch_shapes=scratch,
        compiler_params=pltpu.CompilerParams(dimension_semantics=("arbitrary", "arbitrary"),
                                             vmem_limit_bytes=VMEM_LIMIT),
        name="mixer_ctx" if want_states else "mixer_lat",
    )(*args)
    return outs


def _make_ffn_kernel(grid_conv, seq_len):
    width = GRID_W if grid_conv else seq_len
    dys = (-1, 0, 1) if grid_conv else (0,)
    lane_tiles = FFN_TILE // HEAD_W

    def kernel(x1_ref, mod_ref, n2_ref, wg_ref, wu_ref, cwg_ref, cwu_ref, cbg_ref, cbu_ref, wd_ref, fn_ref,
               y_ref, h_scr, pg_scr, pu_scr, act_scr):
        j = pl.program_id(1)
        copy_of_dx = {-1: 1, 0: 0, 1: 2}

        @pl.when(j == 0)
        def _():
            shift2 = mod_ref[0, :, 3 * D_MODEL:4 * D_MODEL]
            scale2 = mod_ref[0, :, 4 * D_MODEL:5 * D_MODEL]
            n2 = n2_ref[...]

            def norm_rows(rows):
                h = _rms(x1_ref[rows, :]) * n2
                h_scr[rows, :] = (h * (1.0 + scale2) + shift2).astype(BF16)
                y_ref[rows, :] = jnp.zeros((ROW_BLOCK, D_MODEL), F32)
            _row_loop(GROUP_ROWS, ROW_BLOCK, norm_rows)
            zeros = jnp.zeros((CONV_PAD, HEAD_W), F32)
            for scr in (pg_scr, pu_scr):
                for c in range(3 * lane_tiles):
                    scr[c, 0:CONV_PAD, :] = zeros
                    scr[c, CONV_PAD + GROUP_ROWS:2 * CONV_PAD + GROUP_ROWS, :] = zeros

        def up_rows(rows):
            h = h_scr[rows, :]
            dst = pl.ds(_aligned(rows.start + CONV_PAD, 8), MATMUL_ROWS)
            zero_row = jnp.zeros((1, HEAD_W), F32)
            for scr, w_ref in ((pg_scr, wg_ref), (pu_scr, wu_ref)):
                res = jnp.dot(h, w_ref[...], preferred_element_type=F32)
                for lt in range(lane_tiles):
                    part = res[:, lt * HEAD_W:(lt + 1) * HEAD_W]
                    for c in range(3):
                        scr[c * lane_tiles + lt, dst, :] = part
                    for first in range(0, MATMUL_ROWS, width):
                        base = rows.start + CONV_PAD + first
                        scr[lane_tiles + lt, pl.ds(base + width - 1, 1), :] = zero_row
                        scr[2 * lane_tiles + lt, pl.ds(base, 1), :] = zero_row
        _row_loop(GROUP_ROWS, MATMUL_ROWS, up_rows)

        def conv(scr, cw_ref, cb_ref, lt, start):
            lanes = slice(lt * HEAD_W, (lt + 1) * HEAD_W)
            acc = cb_ref[:, lanes]
            for dx in (-1, 0, 1):
                src = copy_of_dx[dx] * lane_tiles + lt
                for dy in dys:
                    tap = cw_ref[(dy + 1) * 3 + (dx + 1):(dy + 1) * 3 + (dx + 2), lanes]
                    acc = acc + scr[src, pl.ds(start + CONV_PAD + dy * GRID_W + dx, ROW_BLOCK), :] * tap
            return acc

        def act_rows(rows):
            for lt in range(lane_tiles):
                cg = conv(pg_scr, cwg_ref, cbg_ref, lt, rows.start)
                cu = conv(pu_scr, cwu_ref, cbu_ref, lt, rows.start)
                act_scr[rows, lt * HEAD_W:(lt + 1) * HEAD_W] = (_silu(cg) * cu).astype(BF16)
        _row_loop(GROUP_ROWS, ROW_BLOCK, act_rows, unroll=2)

        def down_rows(rows):
            y_ref[rows, :] += jnp.dot(act_scr[rows, :], wd_ref[...], preferred_element_type=F32)
        _row_loop(GROUP_ROWS, MATMUL_ROWS, down_rows)

        @pl.when(j == FFN_STEPS - 1)
        def _():
            gate2 = mod_ref[0, :, 5 * D_MODEL:6 * D_MODEL]
            fn = fn_ref[...]

            def res_rows(rows):
                y_ref[rows, :] = _rms(x1_ref[rows, :] + gate2 * y_ref[rows, :]) * fn
            _row_loop(GROUP_ROWS, ROW_BLOCK, res_rows)

    return kernel


def _conv_ffn(x1, mod3, mod_row, norm2, w_up, conv_w, conv_b, w_down, final_norm, grid_conv, seq_len):
    groups = x1.shape[0] // GROUP_ROWS
    once = pl.Buffered(1)
    in_specs = [
        pl.BlockSpec((GROUP_ROWS, D_MODEL), lambda g, j: (g, 0), pipeline_mode=once),
        pl.BlockSpec((1, 1, 6 * D_MODEL), lambda g, j: (mod_row(g), 0, 0)),
        pl.BlockSpec((1, D_MODEL), lambda g, j: (0, 0)),
        pl.BlockSpec((D_MODEL, FFN_TILE), lambda g, j: (0, j)),
        pl.BlockSpec((D_MODEL, FFN_TILE), lambda g, j: (0, FFN_STEPS + j)),
        pl.BlockSpec((9, FFN_TILE), lambda g, j: (0, j)),
        pl.BlockSpec((9, FFN_TILE), lambda g, j: (0, FFN_STEPS + j)),
        pl.BlockSpec((1, FFN_TILE), lambda g, j: (0, j)),
        pl.BlockSpec((1, FFN_TILE), lambda g, j: (0, FFN_STEPS + j)),
        pl.BlockSpec((FFN_TILE, D_MODEL), lambda g, j: (j, 0)),
        pl.BlockSpec((1, D_MODEL), lambda g, j: (0, 0)),
    ]
    lane_tiles = FFN_TILE // HEAD_W
    scratch = [
        pltpu.VMEM((GROUP_ROWS, D_MODEL), BF16),
        pltpu.VMEM((3 * lane_tiles, GROUP_ROWS + 2 * CONV_PAD, HEAD_W), F32),
        pltpu.VMEM((3 * lane_tiles, GROUP_ROWS + 2 * CONV_PAD, HEAD_W), F32),
        pltpu.VMEM((GROUP_ROWS, FFN_TILE), BF16),
    ]
    return pl.pallas_call(
        _make_ffn_kernel(grid_conv, seq_len),
        grid=(groups, FFN_STEPS),
        in_specs=in_specs,
        out_specs=pl.BlockSpec((GROUP_ROWS, D_MODEL), lambda g, j: (g, 0), pipeline_mode=once),
        out_shape=jax.ShapeDtypeStruct(x1.shape, F32),
        scratch_shapes=scratch,
        compiler_params=pltpu.CompilerParams(dimension_semantics=("arbitrary", "arbitrary"),
                                             vmem_limit_bytes=VMEM_LIMIT),
        name="conv_ffn_lat" if grid_conv else "conv_ffn_ctx",
    )(x1, mod3, norm2, w_up, w_up, conv_w, conv_w, conv_b, conv_b, w_down, final_norm)


def _head_layout(w_in, w_gla_up, b_gla, hgrn_lb, gla_norm, hgrn_norm, w_out):
    gla_kw = GLA_HEADS * GLA_DK
    vw = GLA_HEADS * HEAD_W
    o_qa, o_ka, o_va, o_ga = 0, gla_kw, 2 * gla_kw, 2 * gla_kw + vw
    o_lr = o_ga + vw
    o_qb = o_lr + 2 * GLA_LOWRANK
    o_fb = o_qb + vw
    o_ib = o_fb + 2 * vw
    o_gb = o_ib + vw
    wb = w_in.astype(BF16)
    zeros = lambda n: jnp.zeros((D_MODEL, n), BF16)
    heads = []
    for h in range(GLA_HEADS):
        heads.append(jnp.concatenate([
            wb[:, o_qa + h * GLA_DK:o_qa + (h + 1) * GLA_DK], zeros(HEAD_W - GLA_DK),
            wb[:, o_ka + h * GLA_DK:o_ka + (h + 1) * GLA_DK], zeros(HEAD_W - GLA_DK),
            wb[:, o_lr:o_lr + 2 * GLA_LOWRANK], zeros(HEAD_W - 2 * GLA_LOWRANK),
            wb[:, o_va + h * HEAD_W:o_va + (h + 1) * HEAD_W],
            wb[:, o_ga + h * HEAD_W:o_ga + (h + 1) * HEAD_W]], axis=1))
    for h in range(N_HEADS - GLA_HEADS):
        sl = slice(h * HEAD_W, (h + 1) * HEAD_W)
        heads.append(jnp.concatenate([
            wb[:, o_qb:o_qb + vw][:, sl], wb[:, o_fb:o_fb + vw][:, sl], wb[:, o_fb + vw:o_fb + 2 * vw][:, sl],
            wb[:, o_ib:o_ib + vw][:, sl], wb[:, o_gb:o_gb + vw][:, sl]], axis=1))
    w_heads = jnp.stack(heads, axis=0)

    wup = jnp.zeros((N_HEADS, 2, HEAD_W, HEAD_W), BF16)
    bgl = jnp.zeros((N_HEADS, 2, HEAD_W), F32)
    for h in range(GLA_HEADS):
        for d in range(2):
            wup = wup.at[h, d, d * GLA_LOWRANK:(d + 1) * GLA_LOWRANK, 0:GLA_DK].set(
                w_gla_up[d, :, h * GLA_DK:(h + 1) * GLA_DK].astype(BF16))
            bgl = bgl.at[h, d, 0:GLA_DK].set(b_gla[d, h * GLA_DK:(h + 1) * GLA_DK])
    lb = hgrn_lb.astype(F32).reshape(2, 2, N_HEADS - GLA_HEADS, HEAD_W)
    lbl = jnp.concatenate([jnp.zeros((GLA_HEADS, 4, HEAD_W), F32),
                           lb.transpose(2, 0, 1, 3).reshape(N_HEADS - GLA_HEADS, 4, HEAD_W)], axis=0)
    hnorm = jnp.concatenate([jnp.broadcast_to(gla_norm[None, None, :], (GLA_HEADS, 1, HEAD_W)),
                             jnp.broadcast_to(hgrn_norm[None, None, :], (N_HEADS - GLA_HEADS, 1, HEAD_W))], axis=0)
    wout_h = w_out.astype(BF16).reshape(N_HEADS, HEAD_W, D_MODEL)
    return w_heads, wup, bgl, lbl, hnorm, wout_h


def kernel(x_prompt, x_sample, state_gla, state_hgrn, c, c_ctx, w_ada, b_ada, norm1, norm2, w_in, w_gla_up, b_gla, hgrn_lb, gla_norm, hgrn_norm, w_out, w_ffn_up, ffn_conv, b_ffn_conv, w_ffn_down, final_norm):
    assert w_ada.shape[0] == 1 and hgrn_lb.shape[0] == 2, "single layer only"
    n_ctx, ctx_len, _ = x_prompt.shape
    n_lat, lat_len, _ = x_sample.shape
    assert lat_len == GROUP_ROWS and GROUP_ROWS % ctx_len == 0 and n_ctx % (GROUP_ROWS // ctx_len) == 0
    assert n_lat + 1 <= 8
    ctx_per_group = GROUP_ROWS // ctx_len

    cvecs = jnp.concatenate([c_ctx[None, :], c, jnp.zeros((8 - 1 - n_lat, D_MODEL), F32)], axis=0)
    mod3 = _modulation(cvecs, w_ada[0], b_ada).reshape(8, 1, 6 * D_MODEL)

    w_heads, wup, bgl, lbl, hnorm, wout_h = _head_layout(
        w_in[0], w_gla_up[0], b_gla[0], hgrn_lb, gla_norm[0], hgrn_norm[0], w_out[0])
    s0 = jnp.concatenate([jnp.pad(state_gla[:, 0], ((0, 0), (0, 0), (0, 0), (0, HEAD_W - GLA_DK), (0, 0))),
                          state_hgrn[:, 0]], axis=2)
    w_up = w_ffn_up[0].astype(BF16)
    w_down = w_ffn_down[0].astype(BF16)
    conv_w = ffn_conv[0].reshape(9, 2 * FFN_HIDDEN)
    conv_b = b_ffn_conv
    ctx_row = lambda g: 0
    lat_row = lambda g: g + 1

    xp = x_prompt.reshape(n_ctx * ctx_len, D_MODEL)
    xs = x_sample.reshape(n_lat * lat_len, D_MODEL)
    xp1, new_gla, new_hgrn = _mixer(xp, mod3, ctx_row, norm1, w_heads, wup, bgl, lbl, hnorm, wout_h, None,
                                    ctx_per_group, True)
    (xs1,) = _mixer(xs, mod3, lat_row, norm1, w_heads, wup, bgl, lbl, hnorm, wout_h, s0, 1, False)
    yp = _conv_ffn(xp1, mod3, ctx_row, norm2, w_up, conv_w, conv_b, w_down, final_norm[None, :], False, ctx_len)
    ys = _conv_ffn(xs1, mod3, lat_row, norm2, w_up, conv_w, conv_b, w_down, final_norm[None, :], True, lat_len)

    return (yp.reshape(x_prompt.shape), ys.reshape(x_sample.shape), new_gla, new_hgrn)
```

```python
import jax
import jax.numpy as jnp
from jax import lax
from jax.experimental import pallas as pl
from jax.experimental.pallas import tpu as pltpu

F32 = jnp.float32
BF16 = jnp.bfloat16

D_MODEL = 1024
N_HEADS = 8
GLA_HEADS = 4
GLA_DK = 64
HEAD_W = 128
GLA_LOWRANK = 16
GLA_GATE_NORM = 16.0
FFN_HIDDEN = 2816
GRID_W = 64
EPS = 1e-6

GROUP_ROWS = 2048
TILE = 128
CHUNK = 16
CHUNKS_PER_TILE = TILE // CHUNK
TILE_DECAY_LIMIT = 150.0
TILES = GROUP_ROWS // TILE
ROW_BLOCK = 256
MATMUL_ROWS = 1024
PROJ_W = 5 * HEAD_W
FFN_TILE = 256
FFN_STEPS = FFN_HIDDEN // FFN_TILE
CONV_GAP = 8
VMEM_SPARE = 6 * 1024 * 1024

NT_DIMS = (((1,), (1,)), ((), ()))
TN_DIMS = (((0,), (0,)), ((), ()))


def _aligned(x, m):
    return x if isinstance(x, int) else pl.multiple_of(x, m)


def _block_start(i, block):
    return _aligned(i * block, block)


def _vmem_request(blocks, scratch):
    def nbytes(shape, dtype):
        n = jnp.dtype(dtype).itemsize
        for d in shape:
            n *= d
        return n
    total = sum(nbytes(shape, dtype) * bufs for shape, dtype, bufs in blocks)
    total += sum(nbytes(sc.shape, sc.dtype) for sc in scratch)
    return total + VMEM_SPARE


def _row_loop(n_rows, block, body, unroll=1):
    def step(i, carry):
        body(pl.ds(pl.multiple_of(i * block, block), block))
        return carry
    lax.fori_loop(0, n_rows // block, step, 0, unroll=unroll)


def _rms(x):
    return x * lax.rsqrt(jnp.mean(x * x, axis=-1, keepdims=True) + EPS)


def _silu(x):
    return x / (1.0 + jnp.exp(-x))


def _mod_kernel(c_ref, w_ref, b_ref, o_ref):
    a = _silu(c_ref[...]).astype(BF16)
    o_ref[...] = jnp.dot(a, w_ref[...].astype(BF16), preferred_element_type=F32) + b_ref[...]


def _modulation(cvecs, w_ada, b_ada):
    n = w_ada.shape[1]
    tn = 512
    return pl.pallas_call(
        _mod_kernel,
        grid=(n // tn,),
        in_specs=[pl.BlockSpec((8, D_MODEL), lambda j: (0, 0)),
                  pl.BlockSpec((D_MODEL, tn), lambda j: (0, j)),
                  pl.BlockSpec((1, tn), lambda j: (0, j))],
        out_specs=pl.BlockSpec((8, tn), lambda j: (0, j)),
        out_shape=jax.ShapeDtypeStruct((8, n), F32),
        name="modulation",
    )(cvecs, w_ada, b_ada)


def _chunk_phase(q, k, v_bf, g, rev):
    row = lax.broadcasted_iota(jnp.int32, (TILE, HEAD_W), 0)
    col = lax.broadcasted_iota(jnp.int32, (TILE, HEAD_W), 1)
    cpos = row & (CHUNK - 1)
    f = jnp.exp(g)
    if rev:
        fz = jnp.where(cpos == CHUNK - 1, 0.0, f)
        diff = row - col
        edge_row = 0
    else:
        fz = jnp.where(cpos == 0, 0.0, f)
        diff = col - row
        edge_row = CHUNK - 1
    b = g
    for s in (1, 2, 4, 8):
        if rev:
            b = b + jnp.where(cpos <= CHUNK - 1 - s, pltpu.roll(b, TILE - s, 0), 0.0)
        else:
            b = b + jnp.where(cpos >= s, pltpu.roll(b, s, 0), 0.0)
    tot_rows = [b[j * CHUNK + edge_row:j * CHUNK + edge_row + 1, :] for j in range(CHUNKS_PER_TILE)]
    b_tot = jnp.concatenate([jnp.broadcast_to(r, (CHUNK, HEAD_W)) for r in tot_rows], axis=0)
    fc = jnp.exp(jnp.concatenate(tot_rows, axis=0))
    q_in = q * jnp.exp(b)
    k_out = (k * jnp.exp(b_tot - b)).astype(BF16)
    w = k
    a = jnp.zeros((TILE, TILE), F32)
    shift = TILE - 1 if rev else 1
    for d in range(CHUNK):
        if d:
            w = pltpu.roll(w, shift, 0) * fz
        s = jnp.sum(q * w, axis=1, keepdims=True)
        a = jnp.where(diff == -d, s, a)
    o_intra = jnp.dot(a.astype(BF16), v_bf, preferred_element_type=F32)
    return o_intra, q_in, k_out, fc


def _tile_cumsum(g, rev):
    row = lax.broadcasted_iota(jnp.int32, (TILE, TILE), 0)
    col = lax.broadcasted_iota(jnp.int32, (TILE, TILE), 1)
    tri = jnp.where((col >= row) if rev else (col <= row), 1.0, 0.0).astype(BF16)
    hi = g.astype(BF16)
    rest = g - hi.astype(F32)
    mid = rest.astype(BF16)
    lo = (rest - mid.astype(F32)).astype(BF16)
    parts = jnp.dot(tri, jnp.concatenate([hi, mid, lo], axis=1), preferred_element_type=F32)
    return parts[:, 0:HEAD_W] + parts[:, HEAD_W:2 * HEAD_W] + parts[:, 2 * HEAD_W:3 * HEAD_W]


def _tile_phase(q, k, v_bf, b, rev):
    row = lax.broadcasted_iota(jnp.int32, (TILE, TILE), 0)
    col = lax.broadcasted_iota(jnp.int32, (TILE, TILE), 1)
    tot = b[0:1, :] if rev else b[TILE - 1:TILE, :]
    half = 0.5 * tot
    q_s = (q * jnp.exp(b - half)).astype(BF16)
    k_s = (k * jnp.exp(half - b)).astype(BF16)
    a = lax.dot_general(q_s, k_s, NT_DIMS, preferred_element_type=F32)
    a = jnp.where((col >= row) if rev else (col <= row), a, 0.0).astype(BF16)
    o_intra = jnp.dot(a, v_bf, preferred_element_type=F32)
    q_t = (q * jnp.exp(b)).astype(BF16)
    k_out = (k * jnp.exp(tot - b)).astype(BF16)
    s_loc = lax.dot_general(v_bf, k_out, TN_DIMS, preferred_element_type=F32)
    return o_intra, q_t, s_loc, jnp.exp(tot)


def _make_mixer_kernel(seqs_per_group, has_state_in, has_state_out):
    tiles_per_seq = TILES // seqs_per_group

    def kernel(*refs):
        it = iter(refs)
        x_ref = next(it); mod_ref = next(it); n1_ref = next(it); w_ref = next(it)
        wup_ref = next(it); bg_ref = next(it); lb_ref = next(it); hn_ref = next(it); wout_ref = next(it)
        s0_ref = next(it) if has_state_in else None
        x1_ref = next(it)
        sg_ref = next(it) if has_state_out else None
        sh_ref = next(it) if has_state_out else None
        (h_scr, proj_scr, g_scr, k_scr, b_scr, kout_scr, qt_scr, vb_scr, fc_scr, o_scr,
         sloc_scr, gt_scr, st_scr, mg_scr) = it

        hd = pl.program_id(1)

        @pl.when(hd == 0)
        def _():
            shift1 = mod_ref[0, :, 0:D_MODEL]
            scale1 = mod_ref[0, :, D_MODEL:2 * D_MODEL]
            n1 = n1_ref[...]

            def norm_rows(rows):
                h = _rms(x_ref[rows, :]) * n1
                h_scr[rows, :] = (h * (1.0 + scale1) + shift1).astype(BF16)
                x1_ref[rows, :] = jnp.zeros((ROW_BLOCK, D_MODEL), F32)
            _row_loop(GROUP_ROWS, ROW_BLOCK, norm_rows)

        @pl.when(hd < GLA_HEADS)
        def _():
            def proj_rows(rows):
                h = h_scr[rows, :]
                proj_scr[rows, 0:2 * HEAD_W] = jnp.dot(h, w_ref[0, :, 0:2 * HEAD_W], preferred_element_type=F32)
                proj_scr[rows, 3 * HEAD_W:5 * HEAD_W] = jnp.dot(h, w_ref[0, :, 2 * HEAD_W:4 * HEAD_W],
                                                                preferred_element_type=F32)
            _row_loop(GROUP_ROWS, MATMUL_ROWS, proj_rows)

            def gla_rows(rows):
                qk = proj_scr[rows, 0:HEAD_W]
                low = lax.broadcasted_iota(jnp.int32, (ROW_BLOCK, HEAD_W), 1) < GLA_DK
                proj_scr[rows, 0:HEAD_W] = jnp.where(low, qk * (GLA_DK ** -0.5), 0.0)
                k = jnp.where(low, pltpu.roll(qk, HEAD_W - GLA_DK, 1), 0.0)
                lr = proj_scr[rows, HEAD_W:2 * HEAD_W].astype(BF16)
                for d in range(2):
                    z = jnp.dot(lr, wup_ref[0, d], preferred_element_type=F32) + bg_ref[0, d:d + 1, :]
                    log_sig = jnp.minimum(z, 0.0) - jnp.log1p(jnp.exp(-jnp.abs(z)))
                    g_scr[d, rows, :] = log_sig * (1.0 / GLA_GATE_NORM)
                    k_scr[d, rows, :] = k
            _row_loop(GROUP_ROWS, ROW_BLOCK, gla_rows, unroll=2)

        @pl.when(hd >= GLA_HEADS)
        def _():
            def proj_rows(rows):
                proj_scr[rows, :] = jnp.dot(h_scr[rows, :], w_ref[0], preferred_element_type=F32)
            _row_loop(GROUP_ROWS, MATMUL_ROWS, proj_rows)

            def hgrn_rows(rows):
                for d in range(2):
                    a0 = lb_ref[0, d:d + 1, :]
                    a1 = lb_ref[0, 2 + d:3 + d, :]
                    m = jnp.maximum(a0, a1)
                    e0 = jnp.exp(a0 - m)
                    e1 = jnp.exp(a1 - m)
                    lb = e0 / (e0 + e1)
                    xr = proj_scr[rows, (1 + d) * HEAD_W:(2 + d) * HEAD_W]
                    t = jnp.exp(-jnp.abs(xr))
                    big = 1.0 / (1.0 + t)
                    small = t * big
                    pos = xr >= 0.0
                    g_scr[d, rows, :] = jnp.log(lb + (1.0 - lb) * jnp.where(pos, big, small))
                    k_scr[d, rows, :] = (1.0 - lb) * jnp.where(pos, small, big)
            _row_loop(GROUP_ROWS, ROW_BLOCK, hgrn_rows, unroll=2)

        def cum_body(t, lowest):
            rows = pl.ds(pl.multiple_of(t * TILE, TILE), TILE)
            for d in range(2):
                b = _tile_cumsum(g_scr[d, rows, :], rev=(d == 1))
                b_scr[d, rows, :] = b
                lowest = jnp.minimum(lowest, b[0:1, :] if d == 1 else b[TILE - 1:TILE, :])
            return lowest
        lowest = lax.fori_loop(0, TILES, cum_body, jnp.zeros((1, HEAD_W), F32), unroll=TILES)
        tile_safe = jnp.min(lowest) >= -TILE_DECAY_LIMIT

        @pl.when(tile_safe)
        def _():
            def tile_body(t, carry):
                rows = pl.ds(pl.multiple_of(t * TILE, TILE), TILE)
                q = proj_scr[rows, 0:HEAD_W]
                v_bf = proj_scr[rows, 3 * HEAD_W:4 * HEAD_W].astype(BF16)
                o_sum = None
                for d in range(2):
                    o_intra, q_t, s_loc, g_t = _tile_phase(q, k_scr[d, rows, :], v_bf, b_scr[d, rows, :], rev=(d == 1))
                    qt_scr[d, rows, :] = q_t
                    sloc_scr[d, t] = s_loc
                    gt_scr[pl.ds(d * TILES + t, 1), :] = g_t
                    o_sum = o_intra if o_sum is None else o_sum + o_intra
                o_scr[rows, :] = o_sum
                return carry
            lax.fori_loop(0, TILES, tile_body, 0, unroll=TILES)

        @pl.when(jnp.logical_not(tile_safe))
        def _():
            def tile_body(t, carry):
                rows = pl.ds(pl.multiple_of(t * TILE, TILE), TILE)
                q = proj_scr[rows, 0:HEAD_W]
                v_bf = proj_scr[rows, 3 * HEAD_W:4 * HEAD_W].astype(BF16)
                vb_scr[rows, :] = v_bf
                o_sum = None
                for d in range(2):
                    o_intra, q_in, k_out, fc = _chunk_phase(q, k_scr[d, rows, :], v_bf, g_scr[d, rows, :], rev=(d == 1))
                    b_scr[d, rows, :] = q_in
                    kout_scr[d, rows, :] = k_out
                    fc_scr[d, pl.ds(pl.multiple_of(t * CHUNKS_PER_TILE, CHUNKS_PER_TILE), CHUNKS_PER_TILE), :] = fc
                    o_sum = o_intra if o_sum is None else o_sum + o_intra
                o_scr[rows, :] = o_sum
                return carry
            lax.fori_loop(0, TILES, tile_body, 0)

            sloc_scr[...] = jnp.zeros_like(sloc_scr)
            gt_scr[...] = jnp.ones_like(gt_scr)

            def chunk_body(c, carry):
                for d in range(2):
                    cc = c if d == 0 else CHUNKS_PER_TILE - 1 - c
                    for t in range(TILES):
                        rows = pl.ds(pl.multiple_of(t * TILE + cc * CHUNK, CHUNK), CHUNK)
                        q_in = b_scr[d, rows, :]
                        s_loc = sloc_scr[d, t]
                        o_loc = lax.dot_general(q_in.astype(BF16), s_loc.astype(BF16), NT_DIMS,
                                                preferred_element_type=F32)
                        o_scr[rows, :] += o_loc
                        u_t = lax.dot_general(vb_scr[rows, :], kout_scr[d, rows, :], TN_DIMS,
                                              preferred_element_type=F32)
                        fc = fc_scr[d, pl.ds(t * CHUNKS_PER_TILE + cc, 1), :]
                        sloc_scr[d, t] = s_loc * fc + u_t
                        gt = gt_scr[d * TILES + t:d * TILES + t + 1, :]
                        qt_scr[d, rows, :] = (q_in * gt).astype(BF16)
                        gt_scr[d * TILES + t:d * TILES + t + 1, :] = gt * fc
                return carry
            lax.fori_loop(0, CHUNKS_PER_TILE, chunk_body, 0)

        for s in range(seqs_per_group):
            for d in range(2):
                if has_state_in:
                    s_run = s0_ref[s, d, 0].T
                else:
                    s_run = jnp.zeros((HEAD_W, HEAD_W), F32)
                order = range(tiles_per_seq) if d == 0 else range(tiles_per_seq - 1, -1, -1)
                for tt in order:
                    t = s * tiles_per_seq + tt
                    rows = pl.ds(t * TILE, TILE)
                    o_scr[rows, :] += lax.dot_general(qt_scr[d, rows, :], s_run.astype(BF16), NT_DIMS,
                                                      preferred_element_type=F32)
                    s_run = s_run * gt_scr[d * TILES + t:d * TILES + t + 1, :] + sloc_scr[d, t]
                if has_state_out:
                    st_scr[s, d] = s_run.T

        if has_state_out:
            @pl.when(hd < GLA_HEADS)
            def _():
                sg_ref[:, 0, :, 0] = st_scr[:, :, 0:GLA_DK, :]

            @pl.when(hd >= GLA_HEADS)
            def _():
                sh_ref[:, 0, :, 0] = st_scr[...]

        def gated(rows):
            o = _rms(o_scr[rows, :]) * hn_ref[0]
            return (o * _silu(proj_scr[rows, 4 * HEAD_W:5 * HEAD_W])).astype(BF16)

        @pl.when((hd & 1) == 0)
        def _():
            def gate_rows(rows):
                mg_scr[rows, 0:HEAD_W] = gated(rows)
            _row_loop(GROUP_ROWS, ROW_BLOCK, gate_rows, unroll=2)

        @pl.when((hd & 1) == 1)
        def _():
            def gate_rows(rows):
                mg_scr[rows, HEAD_W:2 * HEAD_W] = gated(rows)
            _row_loop(GROUP_ROWS, ROW_BLOCK, gate_rows, unroll=2)

            def out_rows(rows):
                x1_ref[rows, :] += jnp.dot(mg_scr[rows, :], wout_ref[0], preferred_element_type=F32)
            _row_loop(GROUP_ROWS, MATMUL_ROWS, out_rows)

        @pl.when(hd == N_HEADS - 1)
        def _():
            gate1 = mod_ref[0, :, 2 * D_MODEL:3 * D_MODEL]

            def res_rows(rows):
                x1_ref[rows, :] = x_ref[rows, :] + gate1 * x1_ref[rows, :]
            _row_loop(GROUP_ROWS, ROW_BLOCK, res_rows)

    return kernel


def _mixer(x, mod3, mod_row, norm1, w_heads, wup, bgl, lbl, hnorm, wout_h, s0, seqs_per_group, want_states):
    groups = x.shape[0] // GROUP_ROWS
    n_seq = groups * seqs_per_group
    has_state_in = s0 is not None
    once = pl.Buffered(1)
    x_bufs = 1 if want_states else 2
    in_specs = [
        pl.BlockSpec((GROUP_ROWS, D_MODEL), lambda g, h: (g, 0), pipeline_mode=pl.Buffered(x_bufs)),
        pl.BlockSpec((1, 1, 6 * D_MODEL), lambda g, h: (mod_row(g), 0, 0)),
        pl.BlockSpec((1, D_MODEL), lambda g, h: (0, 0)),
        pl.BlockSpec((1, D_MODEL, PROJ_W), lambda g, h: (h, 0, 0)),
        pl.BlockSpec((1, 2, HEAD_W, HEAD_W), lambda g, h: (h, 0, 0, 0)),
        pl.BlockSpec((1, 2, HEAD_W), lambda g, h: (h, 0, 0)),
        pl.BlockSpec((1, 4, HEAD_W), lambda g, h: (h, 0, 0)),
        pl.BlockSpec((1, 1, HEAD_W), lambda g, h: (h, 0, 0)),
        pl.BlockSpec((1, 2 * HEAD_W, D_MODEL), lambda g, h: (h // 2, 0, 0)),
    ]
    args = [x, mod3, norm1, w_heads, wup, bgl, lbl, hnorm, wout_h]
    if has_state_in:
        in_specs.append(pl.BlockSpec((seqs_per_group, 2, 1, HEAD_W, HEAD_W), lambda g, h: (g, 0, h, 0, 0)))
        args.append(s0)
    out_specs = [pl.BlockSpec((GROUP_ROWS, D_MODEL), lambda g, h: (g, 0), pipeline_mode=once)]
    out_shape = [jax.ShapeDtypeStruct(x.shape, F32)]
    if want_states:
        out_specs.append(pl.BlockSpec((seqs_per_group, 1, 2, 1, GLA_DK, HEAD_W),
                                      lambda g, h: (g, 0, 0, jnp.minimum(h, GLA_HEADS - 1), 0, 0)))
        out_specs.append(pl.BlockSpec((seqs_per_group, 1, 2, 1, HEAD_W, HEAD_W),
                                      lambda g, h: (g, 0, 0, jnp.maximum(h - GLA_HEADS, 0), 0, 0)))
        out_shape.append(jax.ShapeDtypeStruct((n_seq, 1, 2, GLA_HEADS, GLA_DK, HEAD_W), F32))
        out_shape.append(jax.ShapeDtypeStruct((n_seq, 1, 2, N_HEADS - GLA_HEADS, HEAD_W, HEAD_W), F32))
    scratch = [
        pltpu.VMEM((GROUP_ROWS, D_MODEL), BF16),
        pltpu.VMEM((GROUP_ROWS, PROJ_W), F32),
        pltpu.VMEM((2, GROUP_ROWS, HEAD_W), F32),
        pltpu.VMEM((2, GROUP_ROWS, HEAD_W), F32),
        pltpu.VMEM((2, GROUP_ROWS, HEAD_W), F32),
        pltpu.VMEM((2, GROUP_ROWS, HEAD_W), BF16),
        pltpu.VMEM((2, GROUP_ROWS, HEAD_W), BF16),
        pltpu.VMEM((GROUP_ROWS, HEAD_W), BF16),
        pltpu.VMEM((2, GROUP_ROWS // CHUNK, HEAD_W), F32),
        pltpu.VMEM((GROUP_ROWS, HEAD_W), F32),
        pltpu.VMEM((2, TILES, HEAD_W, HEAD_W), F32),
        pltpu.VMEM((2 * TILES, HEAD_W), F32),
        pltpu.VMEM((seqs_per_group, 2, HEAD_W, HEAD_W), F32),
        pltpu.VMEM((GROUP_ROWS, 2 * HEAD_W), BF16),
    ]
    state_block = (seqs_per_group, 2, HEAD_W, HEAD_W)
    blocks = [((GROUP_ROWS, D_MODEL), F32, x_bufs), ((GROUP_ROWS, D_MODEL), F32, 1),
              ((D_MODEL, PROJ_W), BF16, 2), ((2 * HEAD_W, D_MODEL), BF16, 2), ((2, HEAD_W, HEAD_W), BF16, 2),
              ((6 * D_MODEL,), F32, 2), (state_block, F32, 2 * (int(has_state_in) + 2 * int(want_states)))]
    outs = pl.pallas_call(
        _make_mixer_kernel(seqs_per_group, has_state_in, want_states),
        grid=(groups, N_HEADS),
        in_specs=in_specs,
        out_specs=out_specs,
        out_shape=out_shape,
        scratch_shapes=scratch,
        compiler_params=pltpu.CompilerParams(dimension_semantics=("arbitrary", "arbitrary"),
                                             vmem_limit_bytes=_vmem_request(blocks, scratch)),
        name="mixer_ctx" if want_states else "mixer_lat",
    )(*args)
    return outs


def _make_ffn_kernel(grid_conv, seq_len):
    width = GRID_W if grid_conv else seq_len
    stride = width + CONV_GAP
    top = stride + CONV_GAP
    dys = (-1, 0, 1) if grid_conv else (0,)
    lane_tiles = FFN_TILE // HEAD_W
    lines_per_mm = MATMUL_ROWS // width
    lines_per_block = max(ROW_BLOCK // width, 1)
    block_rows = lines_per_block * width

    def kernel(x1_ref, mod_ref, n2_ref, wg_ref, wu_ref, cwg_ref, cwu_ref, cbg_ref, cbu_ref, wd_ref, fn_ref,
               y_ref, h_scr, pg_scr, pu_scr, act_scr):
        j = pl.program_id(1)

        @pl.when(j == 0)
        def _():
            shift2 = mod_ref[0, :, 3 * D_MODEL:4 * D_MODEL]
            scale2 = mod_ref[0, :, 4 * D_MODEL:5 * D_MODEL]
            n2 = n2_ref[...]

            def norm_rows(rows):
                h = _rms(x1_ref[rows, :]) * n2
                h_scr[rows, :] = (h * (1.0 + scale2) + shift2).astype(BF16)
                y_ref[rows, :] = jnp.zeros((ROW_BLOCK, D_MODEL), F32)
            _row_loop(GROUP_ROWS, ROW_BLOCK, norm_rows)
            pg_scr[...] = jnp.zeros_like(pg_scr)
            pu_scr[...] = jnp.zeros_like(pu_scr)

        def up_rows(rows):
            h = h_scr[rows, :]
            line0 = rows.start // width
            for scr, w_ref in ((pg_scr, wg_ref), (pu_scr, wu_ref)):
                res = jnp.dot(h, w_ref[...], preferred_element_type=F32)
                for lt in range(lane_tiles):
                    for ln in range(lines_per_mm):
                        dst = pl.ds(_aligned(top + (line0 + ln) * stride, 8), width)
                        scr[lt, dst, :] = res[ln * width:(ln + 1) * width, lt * HEAD_W:(lt + 1) * HEAD_W]
        _row_loop(GROUP_ROWS, MATMUL_ROWS, up_rows)

        def conv(scr, cw_ref, cb_ref, lt, base):
            lanes = slice(lt * HEAD_W, (lt + 1) * HEAD_W)
            acc = cb_ref[:, lanes]
            for dy in dys:
                for dx in (-1, 0, 1):
                    tap = cw_ref[(dy + 1) * 3 + (dx + 1):(dy + 1) * 3 + (dx + 2), lanes]
                    acc = acc + scr[lt, pl.ds(base + dy * stride + dx, width), :] * tap
            return acc

        def act_rows(i, carry):
            for ln in range(lines_per_block):
                line = i * lines_per_block + ln
                base = top + line * stride
                out = pl.ds(pl.multiple_of(line * width, width), width)
                for lt in range(lane_tiles):
                    cg = conv(pg_scr, cwg_ref, cbg_ref, lt, base)
                    cu = conv(pu_scr, cwu_ref, cbu_ref, lt, base)
                    act_scr[out, lt * HEAD_W:(lt + 1) * HEAD_W] = (_silu(cg) * cu).astype(BF16)
            return carry
        lax.fori_loop(0, GROUP_ROWS // block_rows, act_rows, 0, unroll=2)

        def down_rows(rows):
            y_ref[rows, :] += jnp.dot(act_scr[rows, :], wd_ref[...], preferred_element_type=F32)
        _row_loop(GROUP_ROWS, MATMUL_ROWS, down_rows)

        @pl.when(j == FFN_STEPS - 1)
        def _():
            gate2 = mod_ref[0, :, 5 * D_MODEL:6 * D_MODEL]
            fn = fn_ref[...]

            def res_rows(rows):
                y_ref[rows, :] = _rms(x1_ref[rows, :] + gate2 * y_ref[rows, :]) * fn
            _row_loop(GROUP_ROWS, ROW_BLOCK, res_rows)

    return kernel


def _conv_ffn(x1, mod3, mod_row, norm2, w_up, conv_w, conv_b, w_down, final_norm, grid_conv, seq_len):
    groups = x1.shape[0] // GROUP_ROWS
    in_specs = [
        pl.BlockSpec((GROUP_ROWS, D_MODEL), lambda g, j: (g, 0)),
        pl.BlockSpec((1, 1, 6 * D_MODEL), lambda g, j: (mod_row(g), 0, 0)),
        pl.BlockSpec((1, D_MODEL), lambda g, j: (0, 0)),
        pl.BlockSpec((D_MODEL, FFN_TILE), lambda g, j: (0, j)),
        pl.BlockSpec((D_MODEL, FFN_TILE), lambda g, j: (0, FFN_STEPS + j)),
        pl.BlockSpec((9, FFN_TILE), lambda g, j: (0, j)),
        pl.BlockSpec((9, FFN_TILE), lambda g, j: (0, FFN_STEPS + j)),
        pl.BlockSpec((1, FFN_TILE), lambda g, j: (0, j)),
        pl.BlockSpec((1, FFN_TILE), lambda g, j: (0, FFN_STEPS + j)),
        pl.BlockSpec((FFN_TILE, D_MODEL), lambda g, j: (j, 0)),
        pl.BlockSpec((1, D_MODEL), lambda g, j: (0, 0)),
    ]
    lane_tiles = FFN_TILE // HEAD_W
    width = GRID_W if grid_conv else seq_len
    pad_rows = (GROUP_ROWS // width + 2) * (width + CONV_GAP) + 2 * CONV_GAP
    scratch = [
        pltpu.VMEM((GROUP_ROWS, D_MODEL), BF16),
        pltpu.VMEM((lane_tiles, pad_rows, HEAD_W), F32),
        pltpu.VMEM((lane_tiles, pad_rows, HEAD_W), F32),
        pltpu.VMEM((GROUP_ROWS, FFN_TILE), BF16),
    ]
    blocks = [((GROUP_ROWS, D_MODEL), F32, 2), ((GROUP_ROWS, D_MODEL), F32, 2),
              ((D_MODEL, FFN_TILE), BF16, 4), ((FFN_TILE, D_MODEL), BF16, 2), ((16, FFN_TILE), F32, 8),
              ((6 * D_MODEL,), F32, 2)]
    return pl.pallas_call(
        _make_ffn_kernel(grid_conv, seq_len),
        grid=(groups, FFN_STEPS),
        in_specs=in_specs,
        out_specs=pl.BlockSpec((GROUP_ROWS, D_MODEL), lambda g, j: (g, 0)),
        out_shape=jax.ShapeDtypeStruct(x1.shape, F32),
        scratch_shapes=scratch,
        compiler_params=pltpu.CompilerParams(dimension_semantics=("arbitrary", "arbitrary"),
                                             vmem_limit_bytes=_vmem_request(blocks, scratch)),
        name="conv_ffn_lat" if grid_conv else "conv_ffn_ctx",
    )(x1, mod3, norm2, w_up, w_up, conv_w, conv_w, conv_b, conv_b, w_down, final_norm)


def _head_layout(w_in, w_gla_up, b_gla, hgrn_lb, gla_norm, hgrn_norm, w_out):
    gla_kw = GLA_HEADS * GLA_DK
    vw = GLA_HEADS * HEAD_W
    o_qa, o_ka, o_va, o_ga = 0, gla_kw, 2 * gla_kw, 2 * gla_kw + vw
    o_lr = o_ga + vw
    o_qb = o_lr + 2 * GLA_LOWRANK
    o_fb = o_qb + vw
    o_ib = o_fb + 2 * vw
    o_gb = o_ib + vw
    wb = w_in.astype(BF16)
    zeros = lambda n: jnp.zeros((D_MODEL, n), BF16)
    heads = []
    for h in range(GLA_HEADS):
        heads.append(jnp.concatenate([
            wb[:, o_qa + h * GLA_DK:o_qa + (h + 1) * GLA_DK],
            wb[:, o_ka + h * GLA_DK:o_ka + (h + 1) * GLA_DK],
            wb[:, o_lr:o_lr + 2 * GLA_LOWRANK], zeros(HEAD_W - 2 * GLA_LOWRANK),
            wb[:, o_va + h * HEAD_W:o_va + (h + 1) * HEAD_W],
            wb[:, o_ga + h * HEAD_W:o_ga + (h + 1) * HEAD_W],
            zeros(HEAD_W)], axis=1))
    for h in range(N_HEADS - GLA_HEADS):
        sl = slice(h * HEAD_W, (h + 1) * HEAD_W)
        heads.append(jnp.concatenate([
            wb[:, o_qb:o_qb + vw][:, sl], wb[:, o_fb:o_fb + vw][:, sl], wb[:, o_fb + vw:o_fb + 2 * vw][:, sl],
            wb[:, o_ib:o_ib + vw][:, sl], wb[:, o_gb:o_gb + vw][:, sl]], axis=1))
    w_heads = jnp.stack(heads, axis=0)

    wup = jnp.zeros((N_HEADS, 2, HEAD_W, HEAD_W), BF16)
    bgl = jnp.zeros((N_HEADS, 2, HEAD_W), F32)
    for h in range(GLA_HEADS):
        for d in range(2):
            wup = wup.at[h, d, d * GLA_LOWRANK:(d + 1) * GLA_LOWRANK, 0:GLA_DK].set(
                w_gla_up[d, :, h * GLA_DK:(h + 1) * GLA_DK].astype(BF16))
            bgl = bgl.at[h, d, 0:GLA_DK].set(b_gla[d, h * GLA_DK:(h + 1) * GLA_DK])
    lb = hgrn_lb.astype(F32).reshape(2, 2, N_HEADS - GLA_HEADS, HEAD_W)
    lbl = jnp.concatenate([jnp.zeros((GLA_HEADS, 4, HEAD_W), F32),
                           lb.transpose(2, 0, 1, 3).reshape(N_HEADS - GLA_HEADS, 4, HEAD_W)], axis=0)
    hnorm = jnp.concatenate([jnp.broadcast_to(gla_norm[None, None, :], (GLA_HEADS, 1, HEAD_W)),
                             jnp.broadcast_to(hgrn_norm[None, None, :], (N_HEADS - GLA_HEADS, 1, HEAD_W))], axis=0)
    wout_h = w_out.astype(BF16).reshape(N_HEADS // 2, 2 * HEAD_W, D_MODEL)
    return w_heads, wup, bgl, lbl, hnorm, wout_h


def kernel(x_prompt, x_sample, state_gla, state_hgrn, c, c_ctx, w_ada, b_ada, norm1, norm2, w_in, w_gla_up, b_gla, hgrn_lb, gla_norm, hgrn_norm, w_out, w_ffn_up, ffn_conv, b_ffn_conv, w_ffn_down, final_norm):
    assert w_ada.shape[0] == 1 and hgrn_lb.shape[0] == 2, "single layer only"
    n_ctx, ctx_len, _ = x_prompt.shape
    n_lat, lat_len, _ = x_sample.shape
    assert lat_len == GROUP_ROWS and GROUP_ROWS % ctx_len == 0 and n_ctx % (GROUP_ROWS // ctx_len) == 0
    assert n_lat + 1 <= 8
    ctx_per_group = GROUP_ROWS // ctx_len

    cvecs = jnp.concatenate([c_ctx[None, :], c, jnp.zeros((8 - 1 - n_lat, D_MODEL), F32)], axis=0)
    mod3 = _modulation(cvecs, w_ada[0], b_ada).reshape(8, 1, 6 * D_MODEL)

    w_heads, wup, bgl, lbl, hnorm, wout_h = _head_layout(
        w_in[0], w_gla_up[0], b_gla[0], hgrn_lb, gla_norm[0], hgrn_norm[0], w_out[0])
    s0 = jnp.concatenate([jnp.pad(state_gla[:, 0], ((0, 0), (0, 0), (0, 0), (0, HEAD_W - GLA_DK), (0, 0))),
                          state_hgrn[:, 0]], axis=2)
    w_up = w_ffn_up[0].astype(BF16)
    w_down = w_ffn_down[0].astype(BF16)
    conv_w = ffn_conv[0].reshape(9, 2 * FFN_HIDDEN)
    conv_b = b_ffn_conv
    ctx_row = lambda g: 0
    lat_row = lambda g: g + 1

    xp = x_prompt.reshape(n_ctx * ctx_len, D_MODEL)
    xs = x_sample.reshape(n_lat * lat_len, D_MODEL)
    xp1, new_gla, new_hgrn = _mixer(xp, mod3, ctx_row, norm1, w_heads, wup, bgl, lbl, hnorm, wout_h, None,
                                    ctx_per_group, True)
    (xs1,) = _mixer(xs, mod3, lat_row, norm1, w_heads, wup, bgl, lbl, hnorm, wout_h, s0, 1, False)
    yp = _conv_ffn(xp1, mod3, ctx_row, norm2, w_up, conv_w, conv_b, w_down, final_norm[None, :], False, ctx_len)
    ys = _conv_ffn(xs1, mod3, lat_row, norm2, w_up, conv_w, conv_b, w_down, final_norm[None, :], True, lat_len)

    return (yp.reshape(x_prompt.shape), ys.reshape(x_sample.shape), new_gla, new_hgrn)
```

```python
import jax
import jax.numpy as jnp
from jax import lax
from jax.experimental import pallas as pl
from jax.experimental.pallas import tpu as pltpu

F32 = jnp.float32
BF16 = jnp.bfloat16

D_MODEL = 1024
N_HEADS = 8
GLA_HEADS = 4
GLA_DK = 64
HEAD_W = 128
GLA_LOWRANK = 16
GLA_GATE_NORM = 16.0
FFN_HIDDEN = 2816
GRID_W = 64
EPS = 1e-6

GROUP_ROWS = 2048
TILE = 128
CHUNK = 16
CHUNKS_PER_TILE = TILE // CHUNK
TILE_DECAY_LIMIT = 150.0
TILES = GROUP_ROWS // TILE
ROW_BLOCK = 256
MATMUL_ROWS = 1024
PROJ_W = 5 * HEAD_W
FFN_TILE = 256
FFN_STEPS = FFN_HIDDEN // FFN_TILE
CONV_GAP = 8
VMEM_SPARE = 6 * 1024 * 1024

NT_DIMS = (((1,), (1,)), ((), ()))
TN_DIMS = (((0,), (0,)), ((), ()))


def _aligned(x, m):
    return x if isinstance(x, int) else pl.multiple_of(x, m)


def _block_start(i, block):
    return _aligned(i * block, block)


def _vmem_request(blocks, scratch):
    def nbytes(shape, dtype):
        n = jnp.dtype(dtype).itemsize
        for d in shape:
            n *= d
        return n
    total = sum(nbytes(shape, dtype) * bufs for shape, dtype, bufs in blocks)
    total += sum(nbytes(sc.shape, sc.dtype) for sc in scratch)
    return total + VMEM_SPARE


def _row_loop(n_rows, block, body, unroll=1):
    def step(i, carry):
        body(pl.ds(pl.multiple_of(i * block, block), block))
        return carry
    lax.fori_loop(0, n_rows // block, step, 0, unroll=unroll)


def _rms(x):
    return x * lax.rsqrt(jnp.mean(x * x, axis=-1, keepdims=True) + EPS)


def _silu(x):
    return x / (1.0 + jnp.exp(-x))


def _mod_kernel(c_ref, w_ref, b_ref, o_ref):
    a = _silu(c_ref[...]).astype(BF16)
    o_ref[...] = jnp.dot(a, w_ref[...].astype(BF16), preferred_element_type=F32) + b_ref[...]


def _modulation(cvecs, w_ada, b_ada):
    n = w_ada.shape[1]
    tn = 1024
    return pl.pallas_call(
        _mod_kernel,
        grid=(n // tn,),
        in_specs=[pl.BlockSpec((8, D_MODEL), lambda j: (0, 0)),
                  pl.BlockSpec((D_MODEL, tn), lambda j: (0, j)),
                  pl.BlockSpec((1, tn), lambda j: (0, j))],
        out_specs=pl.BlockSpec((8, tn), lambda j: (0, j)),
        out_shape=jax.ShapeDtypeStruct((8, n), F32),
        name="modulation",
    )(cvecs, w_ada, b_ada)


def _chunk_phase(q, k, v_bf, g, rev):
    row = lax.broadcasted_iota(jnp.int32, (TILE, HEAD_W), 0)
    col = lax.broadcasted_iota(jnp.int32, (TILE, HEAD_W), 1)
    cpos = row & (CHUNK - 1)
    f = jnp.exp(g)
    if rev:
        fz = jnp.where(cpos == CHUNK - 1, 0.0, f)
        diff = row - col
        edge_row = 0
    else:
        fz = jnp.where(cpos == 0, 0.0, f)
        diff = col - row
        edge_row = CHUNK - 1
    b = g
    for s in (1, 2, 4, 8):
        if rev:
            b = b + jnp.where(cpos <= CHUNK - 1 - s, pltpu.roll(b, TILE - s, 0), 0.0)
        else:
            b = b + jnp.where(cpos >= s, pltpu.roll(b, s, 0), 0.0)
    tot_rows = [b[j * CHUNK + edge_row:j * CHUNK + edge_row + 1, :] for j in range(CHUNKS_PER_TILE)]
    b_tot = jnp.concatenate([jnp.broadcast_to(r, (CHUNK, HEAD_W)) for r in tot_rows], axis=0)
    fc = jnp.exp(jnp.concatenate(tot_rows, axis=0))
    q_in = q * jnp.exp(b)
    k_out = (k * jnp.exp(b_tot - b)).astype(BF16)
    w = k
    a = jnp.zeros((TILE, TILE), F32)
    shift = TILE - 1 if rev else 1
    for d in range(CHUNK):
        if d:
            w = pltpu.roll(w, shift, 0) * fz
        s = jnp.sum(q * w, axis=1, keepdims=True)
        a = jnp.where(diff == -d, s, a)
    o_intra = jnp.dot(a.astype(BF16), v_bf, preferred_element_type=F32)
    return o_intra, q_in, k_out, fc


def _tile_cumsum(g, rev):
    row = lax.broadcasted_iota(jnp.int32, (TILE, TILE), 0)
    col = lax.broadcasted_iota(jnp.int32, (TILE, TILE), 1)
    tri = jnp.where((col >= row) if rev else (col <= row), 1.0, 0.0).astype(BF16)
    hi = g.astype(BF16)
    rest = g - hi.astype(F32)
    mid = rest.astype(BF16)
    lo = (rest - mid.astype(F32)).astype(BF16)
    parts = jnp.dot(tri, jnp.concatenate([hi, mid, lo], axis=1), preferred_element_type=F32)
    return parts[:, 0:HEAD_W] + parts[:, HEAD_W:2 * HEAD_W] + parts[:, 2 * HEAD_W:3 * HEAD_W]


def _tile_phase(q, k, v_bf, b, rev):
    row = lax.broadcasted_iota(jnp.int32, (TILE, TILE), 0)
    col = lax.broadcasted_iota(jnp.int32, (TILE, TILE), 1)
    tot = b[0:1, :] if rev else b[TILE - 1:TILE, :]
    half = 0.5 * tot
    e_half = jnp.exp(half)
    q_h = q * jnp.exp(b - half)
    k_h = k * jnp.exp(half - b)
    a = lax.dot_general(q_h.astype(BF16), k_h.astype(BF16), NT_DIMS, preferred_element_type=F32)
    a = jnp.where((col >= row) if rev else (col <= row), a, 0.0).astype(BF16)
    o_intra = jnp.dot(a, v_bf, preferred_element_type=F32)
    q_t = (q_h * e_half).astype(BF16)
    k_out = (k_h * e_half).astype(BF16)
    s_loc = lax.dot_general(v_bf, k_out, TN_DIMS, preferred_element_type=F32)
    return o_intra, q_t, s_loc, jnp.exp(tot)


def _make_mixer_kernel(seqs_per_group, has_state_in, has_state_out):
    tiles_per_seq = TILES // seqs_per_group

    def kernel(*refs):
        it = iter(refs)
        x_ref = next(it); mod_ref = next(it); n1_ref = next(it); w_ref = next(it)
        wup_ref = next(it); bg_ref = next(it); lb_ref = next(it); hn_ref = next(it); wout_ref = next(it)
        s0_ref = next(it) if has_state_in else None
        x1_ref = next(it)
        sg_ref = next(it) if has_state_out else None
        sh_ref = next(it) if has_state_out else None
        (h_scr, proj_scr, g_scr, k_scr, b_scr, kout_scr, qt_scr, vb_scr, fc_scr, o_scr,
         sloc_scr, gt_scr, st_scr, mg_scr) = it

        hd = pl.program_id(1)

        @pl.when(hd == 0)
        def _():
            shift1 = mod_ref[0, :, 0:D_MODEL]
            scale1 = mod_ref[0, :, D_MODEL:2 * D_MODEL]
            n1 = n1_ref[...]

            def norm_rows(rows):
                h = _rms(x_ref[rows, :]) * n1
                h_scr[rows, :] = (h * (1.0 + scale1) + shift1).astype(BF16)
                x1_ref[rows, :] = jnp.zeros((ROW_BLOCK, D_MODEL), F32)
            _row_loop(GROUP_ROWS, ROW_BLOCK, norm_rows)

        @pl.when(hd < GLA_HEADS)
        def _():
            def proj_rows(rows):
                h = h_scr[rows, :]
                proj_scr[rows, 0:2 * HEAD_W] = jnp.dot(h, w_ref[0, :, 0:2 * HEAD_W], preferred_element_type=F32)
                proj_scr[rows, 3 * HEAD_W:5 * HEAD_W] = jnp.dot(h, w_ref[0, :, 2 * HEAD_W:4 * HEAD_W],
                                                                preferred_element_type=F32)
            _row_loop(GROUP_ROWS, MATMUL_ROWS, proj_rows)

            def gla_rows(rows):
                qk = proj_scr[rows, 0:HEAD_W]
                low = lax.broadcasted_iota(jnp.int32, (ROW_BLOCK, HEAD_W), 1) < GLA_DK
                proj_scr[rows, 0:HEAD_W] = jnp.where(low, qk * (GLA_DK ** -0.5), 0.0)
                k = jnp.where(low, pltpu.roll(qk, HEAD_W - GLA_DK, 1), 0.0)
                lr = proj_scr[rows, HEAD_W:2 * HEAD_W].astype(BF16)
                for d in range(2):
                    z = jnp.dot(lr, wup_ref[0, d], preferred_element_type=F32) + bg_ref[0, d:d + 1, :]
                    log_sig = jnp.minimum(z, 0.0) - jnp.log(1.0 + jnp.exp(-jnp.abs(z)))
                    g_scr[d, rows, :] = log_sig * (1.0 / GLA_GATE_NORM)
                    k_scr[d, rows, :] = k
            _row_loop(GROUP_ROWS, ROW_BLOCK, gla_rows, unroll=2)

        @pl.when(hd >= GLA_HEADS)
        def _():
            def proj_rows(rows):
                proj_scr[rows, :] = jnp.dot(h_scr[rows, :], w_ref[0], preferred_element_type=F32)
            _row_loop(GROUP_ROWS, MATMUL_ROWS, proj_rows)

            def hgrn_rows(rows):
                for d in range(2):
                    a0 = lb_ref[0, d:d + 1, :]
                    a1 = lb_ref[0, 2 + d:3 + d, :]
                    m = jnp.maximum(a0, a1)
                    e0 = jnp.exp(a0 - m)
                    e1 = jnp.exp(a1 - m)
                    lb = e0 / (e0 + e1)
                    xr = proj_scr[rows, (1 + d) * HEAD_W:(2 + d) * HEAD_W]
                    t = jnp.exp(-jnp.abs(xr))
                    big = 1.0 / (1.0 + t)
                    small = t * big
                    pos = xr >= 0.0
                    g_scr[d, rows, :] = jnp.log(lb + (1.0 - lb) * jnp.where(pos, big, small))
                    k_scr[d, rows, :] = (1.0 - lb) * jnp.where(pos, small, big)
            _row_loop(GROUP_ROWS, ROW_BLOCK, hgrn_rows, unroll=2)

        def cum_body(t, lowest):
            rows = pl.ds(pl.multiple_of(t * TILE, TILE), TILE)
            for d in range(2):
                b = _tile_cumsum(g_scr[d, rows, :], rev=(d == 1))
                b_scr[d, rows, :] = b
                lowest = jnp.minimum(lowest, b[0:1, :] if d == 1 else b[TILE - 1:TILE, :])
            return lowest
        lowest = lax.fori_loop(0, TILES, cum_body, jnp.zeros((1, HEAD_W), F32), unroll=TILES)
        tile_safe = jnp.min(lowest) >= -TILE_DECAY_LIMIT

        @pl.when(tile_safe)
        def _():
            def tile_body(t, carry):
                rows = pl.ds(pl.multiple_of(t * TILE, TILE), TILE)
                q = proj_scr[rows, 0:HEAD_W]
                v_bf = proj_scr[rows, 3 * HEAD_W:4 * HEAD_W].astype(BF16)
                o_sum = None
                for d in range(2):
                    o_intra, q_t, s_loc, g_t = _tile_phase(q, k_scr[d, rows, :], v_bf, b_scr[d, rows, :], rev=(d == 1))
                    qt_scr[d, rows, :] = q_t
                    sloc_scr[d, t] = s_loc
                    gt_scr[pl.ds(d * TILES + t, 1), :] = g_t
                    o_sum = o_intra if o_sum is None else o_sum + o_intra
                o_scr[rows, :] = o_sum
                return carry
            lax.fori_loop(0, TILES, tile_body, 0, unroll=TILES)

        @pl.when(jnp.logical_not(tile_safe))
        def _():
            def tile_body(t, carry):
                rows = pl.ds(pl.multiple_of(t * TILE, TILE), TILE)
                q = proj_scr[rows, 0:HEAD_W]
                v_bf = proj_scr[rows, 3 * HEAD_W:4 * HEAD_W].astype(BF16)
                vb_scr[rows, :] = v_bf
                o_sum = None
                for d in range(2):
                    o_intra, q_in, k_out, fc = _chunk_phase(q, k_scr[d, rows, :], v_bf, g_scr[d, rows, :], rev=(d == 1))
                    b_scr[d, rows, :] = q_in
                    kout_scr[d, rows, :] = k_out
                    fc_scr[d, pl.ds(pl.multiple_of(t * CHUNKS_PER_TILE, CHUNKS_PER_TILE), CHUNKS_PER_TILE), :] = fc
                    o_sum = o_intra if o_sum is None else o_sum + o_intra
                o_scr[rows, :] = o_sum
                return carry
            lax.fori_loop(0, TILES, tile_body, 0)

            sloc_scr[...] = jnp.zeros_like(sloc_scr)
            gt_scr[...] = jnp.ones_like(gt_scr)

            def chunk_body(c, carry):
                for d in range(2):
                    cc = c if d == 0 else CHUNKS_PER_TILE - 1 - c
                    for t in range(TILES):
                        rows = pl.ds(pl.multiple_of(t * TILE + cc * CHUNK, CHUNK), CHUNK)
                        q_in = b_scr[d, rows, :]
                        s_loc = sloc_scr[d, t]
                        o_loc = lax.dot_general(q_in.astype(BF16), s_loc.astype(BF16), NT_DIMS,
                                                preferred_element_type=F32)
                        o_scr[rows, :] += o_loc
                        u_t = lax.dot_general(vb_scr[rows, :], kout_scr[d, rows, :], TN_DIMS,
                                              preferred_element_type=F32)
                        fc = fc_scr[d, pl.ds(t * CHUNKS_PER_TILE + cc, 1), :]
                        sloc_scr[d, t] = s_loc * fc + u_t
                        gt = gt_scr[d * TILES + t:d * TILES + t + 1, :]
                        qt_scr[d, rows, :] = (q_in * gt).astype(BF16)
                        gt_scr[d * TILES + t:d * TILES + t + 1, :] = gt * fc
                return carry
            lax.fori_loop(0, CHUNKS_PER_TILE, chunk_body, 0)

        for s in range(seqs_per_group):
            for d in range(2):
                if has_state_in:
                    s_run = s0_ref[s, d, 0].T
                else:
                    s_run = jnp.zeros((HEAD_W, HEAD_W), F32)
                order = range(tiles_per_seq) if d == 0 else range(tiles_per_seq - 1, -1, -1)
                for tt in order:
                    t = s * tiles_per_seq + tt
                    rows = pl.ds(t * TILE, TILE)
                    o_scr[rows, :] += lax.dot_general(qt_scr[d, rows, :], s_run.astype(BF16), NT_DIMS,
                                                      preferred_element_type=F32)
                    s_run = s_run * gt_scr[d * TILES + t:d * TILES + t + 1, :] + sloc_scr[d, t]
                if has_state_out:
                    st_scr[s, d] = s_run.T

        if has_state_out:
            @pl.when(hd < GLA_HEADS)
            def _():
                sg_ref[:, 0, :, 0] = st_scr[:, :, 0:GLA_DK, :]

            @pl.when(hd >= GLA_HEADS)
            def _():
                sh_ref[:, 0, :, 0] = st_scr[...]

        def gated(rows):
            o = _rms(o_scr[rows, :]) * hn_ref[0]
            return (o * _silu(proj_scr[rows, 4 * HEAD_W:5 * HEAD_W])).astype(BF16)

        @pl.when((hd & 1) == 0)
        def _():
            def gate_rows(rows):
                mg_scr[rows, 0:HEAD_W] = gated(rows)
            _row_loop(GROUP_ROWS, ROW_BLOCK, gate_rows, unroll=2)

        @pl.when((hd & 1) == 1)
        def _():
            def gate_rows(rows):
                mg_scr[rows, HEAD_W:2 * HEAD_W] = gated(rows)
            _row_loop(GROUP_ROWS, ROW_BLOCK, gate_rows, unroll=2)

            def out_rows(rows):
                x1_ref[rows, :] += jnp.dot(mg_scr[rows, :], wout_ref[0], preferred_element_type=F32)
            _row_loop(GROUP_ROWS, MATMUL_ROWS, out_rows)

        @pl.when(hd == N_HEADS - 1)
        def _():
            gate1 = mod_ref[0, :, 2 * D_MODEL:3 * D_MODEL]

            def res_rows(rows):
                x1_ref[rows, :] = x_ref[rows, :] + gate1 * x1_ref[rows, :]
            _row_loop(GROUP_ROWS, ROW_BLOCK, res_rows)

    return kernel


def _mixer(x, mod3, mod_row, norm1, w_heads, wup, bgl, lbl, hnorm, wout_h, s0, seqs_per_group, want_states):
    groups = x.shape[0] // GROUP_ROWS
    n_seq = groups * seqs_per_group
    has_state_in = s0 is not None
    once = pl.Buffered(1)
    x_bufs = 1 if want_states else 2
    in_specs = [
        pl.BlockSpec((GROUP_ROWS, D_MODEL), lambda g, h: (g, 0), pipeline_mode=pl.Buffered(x_bufs)),
        pl.BlockSpec((1, 1, 6 * D_MODEL), lambda g, h: (mod_row(g), 0, 0)),
        pl.BlockSpec((1, D_MODEL), lambda g, h: (0, 0)),
        pl.BlockSpec((1, D_MODEL, PROJ_W), lambda g, h: (h, 0, 0)),
        pl.BlockSpec((1, 2, HEAD_W, HEAD_W), lambda g, h: (h, 0, 0, 0)),
        pl.BlockSpec((1, 2, HEAD_W), lambda g, h: (h, 0, 0)),
        pl.BlockSpec((1, 4, HEAD_W), lambda g, h: (h, 0, 0)),
        pl.BlockSpec((1, 1, HEAD_W), lambda g, h: (h, 0, 0)),
        pl.BlockSpec((1, 2 * HEAD_W, D_MODEL), lambda g, h: (h // 2, 0, 0)),
    ]
    args = [x, mod3, norm1, w_heads, wup, bgl, lbl, hnorm, wout_h]
    if has_state_in:
        in_specs.append(pl.BlockSpec((seqs_per_group, 2, 1, HEAD_W, HEAD_W), lambda g, h: (g, 0, h, 0, 0)))
        args.append(s0)
    out_specs = [pl.BlockSpec((GROUP_ROWS, D_MODEL), lambda g, h: (g, 0), pipeline_mode=once)]
    out_shape = [jax.ShapeDtypeStruct(x.shape, F32)]
    if want_states:
        out_specs.append(pl.BlockSpec((seqs_per_group, 1, 2, 1, GLA_DK, HEAD_W),
                                      lambda g, h: (g, 0, 0, jnp.minimum(h, GLA_HEADS - 1), 0, 0)))
        out_specs.append(pl.BlockSpec((seqs_per_group, 1, 2, 1, HEAD_W, HEAD_W),
                                      lambda g, h: (g, 0, 0, jnp.maximum(h - GLA_HEADS, 0), 0, 0)))
        out_shape.append(jax.ShapeDtypeStruct((n_seq, 1, 2, GLA_HEADS, GLA_DK, HEAD_W), F32))
        out_shape.append(jax.ShapeDtypeStruct((n_seq, 1, 2, N_HEADS - GLA_HEADS, HEAD_W, HEAD_W), F32))
    scratch = [
        pltpu.VMEM((GROUP_ROWS, D_MODEL), BF16),
        pltpu.VMEM((GROUP_ROWS, PROJ_W), F32),
        pltpu.VMEM((2, GROUP_ROWS, HEAD_W), F32),
        pltpu.VMEM((2, GROUP_ROWS, HEAD_W), F32),
        pltpu.VMEM((2, GROUP_ROWS, HEAD_W), F32),
        pltpu.VMEM((2, GROUP_ROWS, HEAD_W), BF16),
        pltpu.VMEM((2, GROUP_ROWS, HEAD_W), BF16),
        pltpu.VMEM((GROUP_ROWS, HEAD_W), BF16),
        pltpu.VMEM((2, GROUP_ROWS // CHUNK, HEAD_W), F32),
        pltpu.VMEM((GROUP_ROWS, HEAD_W), F32),
        pltpu.VMEM((2, TILES, HEAD_W, HEAD_W), F32),
        pltpu.VMEM((2 * TILES, HEAD_W), F32),
        pltpu.VMEM((seqs_per_group, 2, HEAD_W, HEAD_W), F32),
        pltpu.VMEM((GROUP_ROWS, 2 * HEAD_W), BF16),
    ]
    state_block = (seqs_per_group, 2, HEAD_W, HEAD_W)
    blocks = [((GROUP_ROWS, D_MODEL), F32, x_bufs), ((GROUP_ROWS, D_MODEL), F32, 1),
              ((D_MODEL, PROJ_W), BF16, 2), ((2 * HEAD_W, D_MODEL), BF16, 2), ((2, HEAD_W, HEAD_W), BF16, 2),
              ((6 * D_MODEL,), F32, 2), (state_block, F32, 2 * (int(has_state_in) + 2 * int(want_states)))]
    outs = pl.pallas_call(
        _make_mixer_kernel(seqs_per_group, has_state_in, want_states),
        grid=(groups, N_HEADS),
        in_specs=in_specs,
        out_specs=out_specs,
        out_shape=out_shape,
        scratch_shapes=scratch,
        compiler_params=pltpu.CompilerParams(dimension_semantics=("arbitrary", "arbitrary"),
                                             vmem_limit_bytes=_vmem_request(blocks, scratch)),
        name="mixer_ctx" if want_states else "mixer_lat",
    )(*args)
    return outs


def _make_ffn_kernel(grid_conv, seq_len):
    width = GRID_W if grid_conv else seq_len
    stride = width + CONV_GAP
    top = stride + CONV_GAP
    dys = (-1, 0, 1) if grid_conv else (0,)
    lane_tiles = FFN_TILE // HEAD_W
    lines_per_mm = MATMUL_ROWS // width
    lines_per_block = max(ROW_BLOCK // width, 1)
    block_rows = lines_per_block * width

    def kernel(x1_ref, mod_ref, n2_ref, wg_ref, wu_ref, cwg_ref, cwu_ref, cbg_ref, cbu_ref, wd_ref, fn_ref,
               y_ref, h_scr, pg_scr, pu_scr, act_scr):
        j = pl.program_id(1)

        @pl.when(j == 0)
        def _():
            shift2 = mod_ref[0, :, 3 * D_MODEL:4 * D_MODEL]
            scale2 = mod_ref[0, :, 4 * D_MODEL:5 * D_MODEL]
            n2 = n2_ref[...]

            def norm_rows(rows):
                h = _rms(x1_ref[rows, :]) * n2
                h_scr[rows, :] = (h * (1.0 + scale2) + shift2).astype(BF16)
                y_ref[rows, :] = jnp.zeros((ROW_BLOCK, D_MODEL), F32)
            _row_loop(GROUP_ROWS, ROW_BLOCK, norm_rows)
            pg_scr[...] = jnp.zeros_like(pg_scr)
            pu_scr[...] = jnp.zeros_like(pu_scr)

        def up_rows(rows):
            h = h_scr[rows, :]
            line0 = rows.start // width
            for scr, w_ref in ((pg_scr, wg_ref), (pu_scr, wu_ref)):
                res = jnp.dot(h, w_ref[...], preferred_element_type=F32)
                for lt in range(lane_tiles):
                    for ln in range(lines_per_mm):
                        dst = pl.ds(_aligned(top + (line0 + ln) * stride, 8), width)
                        scr[lt, dst, :] = res[ln * width:(ln + 1) * width, lt * HEAD_W:(lt + 1) * HEAD_W]
        _row_loop(GROUP_ROWS, MATMUL_ROWS, up_rows)

        def conv(scr, cw_ref, cb_ref, lt, base):
            lanes = slice(lt * HEAD_W, (lt + 1) * HEAD_W)
            acc = cb_ref[:, lanes]
            for dy in dys:
                for dx in (-1, 0, 1):
                    tap = cw_ref[(dy + 1) * 3 + (dx + 1):(dy + 1) * 3 + (dx + 2), lanes]
                    acc = acc + scr[lt, pl.ds(base + dy * stride + dx, width), :] * tap
            return acc

        def act_rows(i, carry):
            for ln in range(lines_per_block):
                line = i * lines_per_block + ln
                base = top + line * stride
                out = pl.ds(pl.multiple_of(line * width, width), width)
                for lt in range(lane_tiles):
                    cg = conv(pg_scr, cwg_ref, cbg_ref, lt, base)
                    cu = conv(pu_scr, cwu_ref, cbu_ref, lt, base)
                    act_scr[out, lt * HEAD_W:(lt + 1) * HEAD_W] = (_silu(cg) * cu).astype(BF16)
            return carry
        lax.fori_loop(0, GROUP_ROWS // block_rows, act_rows, 0, unroll=2)

        def down_rows(rows):
            y_ref[rows, :] += jnp.dot(act_scr[rows, :], wd_ref[...], preferred_element_type=F32)
        _row_loop(GROUP_ROWS, MATMUL_ROWS, down_rows)

        @pl.when(j == FFN_STEPS - 1)
        def _():
            gate2 = mod_ref[0, :, 5 * D_MODEL:6 * D_MODEL]
            fn = fn_ref[...]

            def res_rows(rows):
                y_ref[rows, :] = _rms(x1_ref[rows, :] + gate2 * y_ref[rows, :]) * fn
            _row_loop(GROUP_ROWS, ROW_BLOCK, res_rows)

    return kernel


def _conv_ffn(x1, mod3, mod_row, norm2, w_up, conv_w, conv_b, w_down, final_norm, grid_conv, seq_len):
    groups = x1.shape[0] // GROUP_ROWS
    in_specs = [
        pl.BlockSpec((GROUP_ROWS, D_MODEL), lambda g, j: (g, 0)),
        pl.BlockSpec((1, 1, 6 * D_MODEL), lambda g, j: (mod_row(g), 0, 0)),
        pl.BlockSpec((1, D_MODEL), lambda g, j: (0, 0)),
        pl.BlockSpec((D_MODEL, FFN_TILE), lambda g, j: (0, j)),
        pl.BlockSpec((D_MODEL, FFN_TILE), lambda g, j: (0, FFN_STEPS + j)),
        pl.BlockSpec((9, FFN_TILE), lambda g, j: (0, j)),
        pl.BlockSpec((9, FFN_TILE), lambda g, j: (0, FFN_STEPS + j)),
        pl.BlockSpec((1, FFN_TILE), lambda g, j: (0, j)),
        pl.BlockSpec((1, FFN_TILE), lambda g, j: (0, FFN_STEPS + j)),
        pl.BlockSpec((FFN_TILE, D_MODEL), lambda g, j: (j, 0)),
        pl.BlockSpec((1, D_MODEL), lambda g, j: (0, 0)),
    ]
    lane_tiles = FFN_TILE // HEAD_W
    width = GRID_W if grid_conv else seq_len
    pad_rows = (GROUP_ROWS // width + 2) * (width + CONV_GAP) + 2 * CONV_GAP
    scratch = [
        pltpu.VMEM((GROUP_ROWS, D_MODEL), BF16),
        pltpu.VMEM((lane_tiles, pad_rows, HEAD_W), F32),
        pltpu.VMEM((lane_tiles, pad_rows, HEAD_W), F32),
        pltpu.VMEM((GROUP_ROWS, FFN_TILE), BF16),
    ]
    blocks = [((GROUP_ROWS, D_MODEL), F32, 2), ((GROUP_ROWS, D_MODEL), F32, 2),
              ((D_MODEL, FFN_TILE), BF16, 4), ((FFN_TILE, D_MODEL), BF16, 2), ((16, FFN_TILE), F32, 8),
              ((6 * D_MODEL,), F32, 2)]
    return pl.pallas_call(
        _make_ffn_kernel(grid_conv, seq_len),
        grid=(groups, FFN_STEPS),
        in_specs=in_specs,
        out_specs=pl.BlockSpec((GROUP_ROWS, D_MODEL), lambda g, j: (g, 0)),
        out_shape=jax.ShapeDtypeStruct(x1.shape, F32),
        scratch_shapes=scratch,
        compiler_params=pltpu.CompilerParams(dimension_semantics=("arbitrary", "arbitrary"),
                                             vmem_limit_bytes=_vmem_request(blocks, scratch)),
        name="conv_ffn_lat" if grid_conv else "conv_ffn_ctx",
    )(x1, mod3, norm2, w_up, w_up, conv_w, conv_w, conv_b, conv_b, w_down, final_norm)


def _head_layout(w_in, w_gla_up, b_gla, hgrn_lb, gla_norm, hgrn_norm, w_out):
    gla_kw = GLA_HEADS * GLA_DK
    vw = GLA_HEADS * HEAD_W
    o_qa, o_ka, o_va, o_ga = 0, gla_kw, 2 * gla_kw, 2 * gla_kw + vw
    o_lr = o_ga + vw
    o_qb = o_lr + 2 * GLA_LOWRANK
    o_fb = o_qb + vw
    o_ib = o_fb + 2 * vw
    o_gb = o_ib + vw
    wb = w_in.astype(BF16)
    zeros = lambda n: jnp.zeros((D_MODEL, n), BF16)
    heads = []
    for h in range(GLA_HEADS):
        heads.append(jnp.concatenate([
            wb[:, o_qa + h * GLA_DK:o_qa + (h + 1) * GLA_DK],
            wb[:, o_ka + h * GLA_DK:o_ka + (h + 1) * GLA_DK],
            wb[:, o_lr:o_lr + 2 * GLA_LOWRANK], zeros(HEAD_W - 2 * GLA_LOWRANK),
            wb[:, o_va + h * HEAD_W:o_va + (h + 1) * HEAD_W],
            wb[:, o_ga + h * HEAD_W:o_ga + (h + 1) * HEAD_W],
            zeros(HEAD_W)], axis=1))
    for h in range(N_HEADS - GLA_HEADS):
        sl = slice(h * HEAD_W, (h + 1) * HEAD_W)
        heads.append(jnp.concatenate([
            wb[:, o_qb:o_qb + vw][:, sl], wb[:, o_fb:o_fb + vw][:, sl], wb[:, o_fb + vw:o_fb + 2 * vw][:, sl],
            wb[:, o_ib:o_ib + vw][:, sl], wb[:, o_gb:o_gb + vw][:, sl]], axis=1))
    w_heads = jnp.stack(heads, axis=0)

    up = w_gla_up.astype(BF16).reshape(2, GLA_LOWRANK, GLA_HEADS, GLA_DK).transpose(2, 0, 1, 3)
    wup = jnp.stack([jnp.pad(up[:, d], ((0, N_HEADS - GLA_HEADS), (d * GLA_LOWRANK, HEAD_W - (d + 1) * GLA_LOWRANK),
                                         (0, HEAD_W - GLA_DK))) for d in range(2)], axis=1)
    bgl = jnp.pad(b_gla.reshape(2, GLA_HEADS, GLA_DK).transpose(1, 0, 2),
                  ((0, N_HEADS - GLA_HEADS), (0, 0), (0, HEAD_W - GLA_DK)))
    lb = hgrn_lb.astype(F32).reshape(2, 2, N_HEADS - GLA_HEADS, HEAD_W)
    lbl = jnp.concatenate([jnp.zeros((GLA_HEADS, 4, HEAD_W), F32),
                           lb.transpose(2, 0, 1, 3).reshape(N_HEADS - GLA_HEADS, 4, HEAD_W)], axis=0)
    hnorm = jnp.concatenate([jnp.broadcast_to(gla_norm[None, None, :], (GLA_HEADS, 1, HEAD_W)),
                             jnp.broadcast_to(hgrn_norm[None, None, :], (N_HEADS - GLA_HEADS, 1, HEAD_W))], axis=0)
    wout_h = w_out.astype(BF16).reshape(N_HEADS // 2, 2 * HEAD_W, D_MODEL)
    return w_heads, wup, bgl, lbl, hnorm, wout_h


def kernel(x_prompt, x_sample, state_gla, state_hgrn, c, c_ctx, w_ada, b_ada, norm1, norm2, w_in, w_gla_up, b_gla, hgrn_lb, gla_norm, hgrn_norm, w_out, w_ffn_up, ffn_conv, b_ffn_conv, w_ffn_down, final_norm):
    assert w_ada.shape[0] == 1 and hgrn_lb.shape[0] == 2, "single layer only"
    n_ctx, ctx_len, _ = x_prompt.shape
    n_lat, lat_len, _ = x_sample.shape
    assert lat_len == GROUP_ROWS and GROUP_ROWS % ctx_len == 0 and n_ctx % (GROUP_ROWS // ctx_len) == 0
    assert n_lat + 1 <= 8
    ctx_per_group = GROUP_ROWS // ctx_len

    cvecs = jnp.concatenate([c_ctx[None, :], c, jnp.zeros((8 - 1 - n_lat, D_MODEL), F32)], axis=0)
    mod3 = _modulation(cvecs, w_ada[0], b_ada).reshape(8, 1, 6 * D_MODEL)

    w_heads, wup, bgl, lbl, hnorm, wout_h = _head_layout(
        w_in[0], w_gla_up[0], b_gla[0], hgrn_lb, gla_norm[0], hgrn_norm[0], w_out[0])
    s0 = jnp.concatenate([jnp.pad(state_gla[:, 0], ((0, 0), (0, 0), (0, 0), (0, HEAD_W - GLA_DK), (0, 0))),
                          state_hgrn[:, 0]], axis=2)
    w_up = w_ffn_up[0].astype(BF16)
    w_down = w_ffn_down[0].astype(BF16)
    conv_w = ffn_conv[0].reshape(9, 2 * FFN_HIDDEN)
    conv_b = b_ffn_conv
    ctx_row = lambda g: 0
    lat_row = lambda g: g + 1

    xp = x_prompt.reshape(n_ctx * ctx_len, D_MODEL)
    xs = x_sample.reshape(n_lat * lat_len, D_MODEL)
    xp1, new_gla, new_hgrn = _mixer(xp, mod3, ctx_row, norm1, w_heads, wup, bgl, lbl, hnorm, wout_h, None,
                                    ctx_per_group, True)
    (xs1,) = _mixer(xs, mod3, lat_row, norm1, w_heads, wup, bgl, lbl, hnorm, wout_h, s0, 1, False)
    yp = _conv_ffn(xp1, mod3, ctx_row, norm2, w_up, conv_w, conv_b, w_down, final_norm[None, :], False, ctx_len)
    ys = _conv_ffn(xs1, mod3, lat_row, norm2, w_up, conv_w, conv_b, w_down, final_norm[None, :], True, lat_len)

    return (yp.reshape(x_prompt.shape), ys.reshape(x_sample.shape), new_gla, new_hgrn)
```

```python
import jax
import jax.numpy as jnp
from jax import lax
from jax.experimental import pallas as pl
from jax.experimental.pallas import tpu as pltpu

F32 = jnp.float32
BF16 = jnp.bfloat16

D_MODEL = 1024
N_HEADS = 8
GLA_HEADS = 4
GLA_DK = 64
HEAD_W = 128
GLA_LOWRANK = 16
GLA_GATE_NORM = 16.0
FFN_HIDDEN = 2816
GRID_W = 64
EPS = 1e-6

GROUP_ROWS = 2048
TILE = 128
CHUNK = 16
CHUNKS_PER_TILE = TILE // CHUNK
TILE_DECAY_LIMIT = 150.0
TILES = GROUP_ROWS // TILE
ROW_BLOCK = 256
MATMUL_ROWS = 1024
PROJ_W = 5 * HEAD_W
FFN_TILE = 256
FFN_STEPS = FFN_HIDDEN // FFN_TILE
CONV_GAP = 8
VMEM_SPARE = 6 * 1024 * 1024

NT_DIMS = (((1,), (1,)), ((), ()))
TN_DIMS = (((0,), (0,)), ((), ()))


def _aligned(x, m):
    return x if isinstance(x, int) else pl.multiple_of(x, m)


def _block_start(i, block):
    return _aligned(i * block, block)


def _vmem_request(blocks, scratch):
    def nbytes(shape, dtype):
        n = jnp.dtype(dtype).itemsize
        for d in shape:
            n *= d
        return n
    total = sum(nbytes(shape, dtype) * bufs for shape, dtype, bufs in blocks)
    total += sum(nbytes(sc.shape, sc.dtype) for sc in scratch)
    return total + VMEM_SPARE


def _row_loop(n_rows, block, body, unroll=1):
    def step(i, carry):
        body(pl.ds(pl.multiple_of(i * block, block), block))
        return carry
    lax.fori_loop(0, n_rows // block, step, 0, unroll=unroll)


def _rms(x):
    return x * lax.rsqrt(jnp.mean(x * x, axis=-1, keepdims=True) + EPS)


def _silu(x):
    return x / (1.0 + jnp.exp(-x))


def _mod_kernel(c_ref, w_ref, b_ref, o_ref):
    a = _silu(c_ref[...]).astype(BF16)
    o_ref[...] = jnp.dot(a, w_ref[...].astype(BF16), preferred_element_type=F32) + b_ref[...]


def _modulation(cvecs, w_ada, b_ada):
    n = w_ada.shape[1]
    tn = 1024
    return pl.pallas_call(
        _mod_kernel,
        grid=(n // tn,),
        in_specs=[pl.BlockSpec((8, D_MODEL), lambda j: (0, 0)),
                  pl.BlockSpec((D_MODEL, tn), lambda j: (0, j)),
                  pl.BlockSpec((1, tn), lambda j: (0, j))],
        out_specs=pl.BlockSpec((8, tn), lambda j: (0, j)),
        out_shape=jax.ShapeDtypeStruct((8, n), F32),
        name="modulation",
    )(cvecs, w_ada, b_ada)


def _chunk_phase(q, k, v_bf, g, rev):
    row = lax.broadcasted_iota(jnp.int32, (TILE, HEAD_W), 0)
    col = lax.broadcasted_iota(jnp.int32, (TILE, HEAD_W), 1)
    cpos = row & (CHUNK - 1)
    f = jnp.exp(g)
    if rev:
        fz = jnp.where(cpos == CHUNK - 1, 0.0, f)
        diff = row - col
        edge_row = 0
    else:
        fz = jnp.where(cpos == 0, 0.0, f)
        diff = col - row
        edge_row = CHUNK - 1
    b = g
    for s in (1, 2, 4, 8):
        if rev:
            b = b + jnp.where(cpos <= CHUNK - 1 - s, pltpu.roll(b, TILE - s, 0), 0.0)
        else:
            b = b + jnp.where(cpos >= s, pltpu.roll(b, s, 0), 0.0)
    tot_rows = [b[j * CHUNK + edge_row:j * CHUNK + edge_row + 1, :] for j in range(CHUNKS_PER_TILE)]
    b_tot = jnp.concatenate([jnp.broadcast_to(r, (CHUNK, HEAD_W)) for r in tot_rows], axis=0)
    fc = jnp.exp(jnp.concatenate(tot_rows, axis=0))
    q_in = q * jnp.exp(b)
    k_out = (k * jnp.exp(b_tot - b)).astype(BF16)
    w = k
    a = jnp.zeros((TILE, TILE), F32)
    shift = TILE - 1 if rev else 1
    for d in range(CHUNK):
        if d:
            w = pltpu.roll(w, shift, 0) * fz
        s = jnp.sum(q * w, axis=1, keepdims=True)
        a = jnp.where(diff == -d, s, a)
    o_intra = jnp.dot(a.astype(BF16), v_bf, preferred_element_type=F32)
    return o_intra, q_in, k_out, fc


def _tile_cumsum(g, rev):
    row = lax.broadcasted_iota(jnp.int32, (TILE, TILE), 0)
    col = lax.broadcasted_iota(jnp.int32, (TILE, TILE), 1)
    tri = jnp.where((col >= row) if rev else (col <= row), 1.0, 0.0).astype(BF16)
    hi = g.astype(BF16)
    rest = g - hi.astype(F32)
    mid = rest.astype(BF16)
    lo = (rest - mid.astype(F32)).astype(BF16)
    parts = jnp.dot(tri, jnp.concatenate([hi, mid, lo], axis=1), preferred_element_type=F32)
    return parts[:, 0:HEAD_W] + parts[:, HEAD_W:2 * HEAD_W] + parts[:, 2 * HEAD_W:3 * HEAD_W]


def _tile_phase(q, k, v_bf, b, rev):
    row = lax.broadcasted_iota(jnp.int32, (TILE, TILE), 0)
    col = lax.broadcasted_iota(jnp.int32, (TILE, TILE), 1)
    tot = b[0:1, :] if rev else b[TILE - 1:TILE, :]
    half = 0.5 * tot
    e_half = jnp.exp(half)
    q_h = q * jnp.exp(b - half)
    k_h = k * jnp.exp(half - b)
    a = lax.dot_general(q_h.astype(BF16), k_h.astype(BF16), NT_DIMS, preferred_element_type=F32)
    a = jnp.where((col >= row) if rev else (col <= row), a, 0.0).astype(BF16)
    o_intra = jnp.dot(a, v_bf, preferred_element_type=F32)
    q_t = (q_h * e_half).astype(BF16)
    k_out = (k_h * e_half).astype(BF16)
    s_loc = lax.dot_general(v_bf, k_out, TN_DIMS, preferred_element_type=F32)
    return o_intra, q_t, s_loc, jnp.exp(tot)


def _make_mixer_kernel(seqs_per_group, has_state_in, has_state_out):
    tiles_per_seq = TILES // seqs_per_group

    def kernel(*refs):
        it = iter(refs)
        x_ref = next(it); mod_ref = next(it); n1_ref = next(it); w_ref = next(it)
        wup_ref = next(it); bg_ref = next(it); lb_ref = next(it); hn_ref = next(it); wout_ref = next(it)
        s0_ref = next(it) if has_state_in else None
        x1_ref = next(it)
        sg_ref = next(it) if has_state_out else None
        sh_ref = next(it) if has_state_out else None
        (h_scr, proj_scr, g_scr, k_scr, b_scr, kout_scr, qt_scr, vb_scr, fc_scr, o_scr,
         sloc_scr, gt_scr, st_scr, mg_scr) = it

        hd = pl.program_id(1)

        @pl.when(hd == 0)
        def _():
            shift1 = mod_ref[0, :, 0:D_MODEL]
            scale1 = mod_ref[0, :, D_MODEL:2 * D_MODEL]
            n1 = n1_ref[...]

            def norm_rows(rows):
                h = _rms(x_ref[rows, :]) * n1
                h_scr[rows, :] = (h * (1.0 + scale1) + shift1).astype(BF16)
                x1_ref[rows, :] = jnp.zeros((ROW_BLOCK, D_MODEL), F32)
            _row_loop(GROUP_ROWS, ROW_BLOCK, norm_rows)

        @pl.when(hd < GLA_HEADS)
        def _():
            def proj_rows(rows):
                h = h_scr[rows, :]
                proj_scr[rows, 0:2 * HEAD_W] = jnp.dot(h, w_ref[:, 0:2 * HEAD_W], preferred_element_type=F32)
                proj_scr[rows, 3 * HEAD_W:5 * HEAD_W] = jnp.dot(h, w_ref[:, 2 * HEAD_W:4 * HEAD_W],
                                                                preferred_element_type=F32)
            _row_loop(GROUP_ROWS, MATMUL_ROWS, proj_rows)

            def gla_rows(rows):
                qk = proj_scr[rows, 0:HEAD_W]
                low = lax.broadcasted_iota(jnp.int32, (ROW_BLOCK, HEAD_W), 1) < GLA_DK
                proj_scr[rows, 0:HEAD_W] = jnp.where(low, qk * (GLA_DK ** -0.5), 0.0)
                k = jnp.where(low, pltpu.roll(qk, HEAD_W - GLA_DK, 1), 0.0)
                lr = proj_scr[rows, HEAD_W:2 * HEAD_W].astype(BF16)
                for d in range(2):
                    z = jnp.dot(lr, wup_ref[0, d], preferred_element_type=F32) + bg_ref[0, d:d + 1, :]
                    log_sig = jnp.minimum(z, 0.0) - jnp.log(1.0 + jnp.exp(-jnp.abs(z)))
                    g_scr[d, rows, :] = log_sig * (1.0 / GLA_GATE_NORM)
                    k_scr[d, rows, :] = k
            _row_loop(GROUP_ROWS, ROW_BLOCK, gla_rows, unroll=2)

        @pl.when(hd >= GLA_HEADS)
        def _():
            def proj_rows(rows):
                proj_scr[rows, :] = jnp.dot(h_scr[rows, :], w_ref[...], preferred_element_type=F32)
            _row_loop(GROUP_ROWS, MATMUL_ROWS, proj_rows)

            def hgrn_rows(rows):
                for d in range(2):
                    a0 = lb_ref[0, d:d + 1, :]
                    a1 = lb_ref[0, 2 + d:3 + d, :]
                    m = jnp.maximum(a0, a1)
                    e0 = jnp.exp(a0 - m)
                    e1 = jnp.exp(a1 - m)
                    lb = e0 / (e0 + e1)
                    xr = proj_scr[rows, (1 + d) * HEAD_W:(2 + d) * HEAD_W]
                    t = jnp.exp(-jnp.abs(xr))
                    big = 1.0 / (1.0 + t)
                    small = t * big
                    pos = xr >= 0.0
                    g_scr[d, rows, :] = jnp.log(lb + (1.0 - lb) * jnp.where(pos, big, small))
                    k_scr[d, rows, :] = (1.0 - lb) * jnp.where(pos, small, big)
            _row_loop(GROUP_ROWS, ROW_BLOCK, hgrn_rows, unroll=2)

        def cum_body(t, lowest):
            rows = pl.ds(pl.multiple_of(t * TILE, TILE), TILE)
            for d in range(2):
                b = _tile_cumsum(g_scr[d, rows, :], rev=(d == 1))
                b_scr[d, rows, :] = b
                lowest = jnp.minimum(lowest, b[0:1, :] if d == 1 else b[TILE - 1:TILE, :])
            return lowest
        lowest = lax.fori_loop(0, TILES, cum_body, jnp.zeros((1, HEAD_W), F32), unroll=TILES)
        tile_safe = jnp.min(lowest) >= -TILE_DECAY_LIMIT

        @pl.when(tile_safe)
        def _():
            def tile_body(t, carry):
                rows = pl.ds(pl.multiple_of(t * TILE, TILE), TILE)
                q = proj_scr[rows, 0:HEAD_W]
                v_bf = proj_scr[rows, 3 * HEAD_W:4 * HEAD_W].astype(BF16)
                o_sum = None
                for d in range(2):
                    o_intra, q_t, s_loc, g_t = _tile_phase(q, k_scr[d, rows, :], v_bf, b_scr[d, rows, :], rev=(d == 1))
                    qt_scr[d, rows, :] = q_t
                    sloc_scr[d, t] = s_loc
                    gt_scr[pl.ds(d * TILES + t, 1), :] = g_t
                    o_sum = o_intra if o_sum is None else o_sum + o_intra
                o_scr[rows, :] = o_sum
                return carry
            lax.fori_loop(0, TILES, tile_body, 0, unroll=TILES)

        @pl.when(jnp.logical_not(tile_safe))
        def _():
            def tile_body(t, carry):
                rows = pl.ds(pl.multiple_of(t * TILE, TILE), TILE)
                q = proj_scr[rows, 0:HEAD_W]
                v_bf = proj_scr[rows, 3 * HEAD_W:4 * HEAD_W].astype(BF16)
                vb_scr[rows, :] = v_bf
                o_sum = None
                for d in range(2):
                    o_intra, q_in, k_out, fc = _chunk_phase(q, k_scr[d, rows, :], v_bf, g_scr[d, rows, :], rev=(d == 1))
                    b_scr[d, rows, :] = q_in
                    kout_scr[d, rows, :] = k_out
                    fc_scr[d, pl.ds(pl.multiple_of(t * CHUNKS_PER_TILE, CHUNKS_PER_TILE), CHUNKS_PER_TILE), :] = fc
                    o_sum = o_intra if o_sum is None else o_sum + o_intra
                o_scr[rows, :] = o_sum
                return carry
            lax.fori_loop(0, TILES, tile_body, 0)

            sloc_scr[...] = jnp.zeros_like(sloc_scr)
            gt_scr[...] = jnp.ones_like(gt_scr)

            def chunk_body(c, carry):
                for d in range(2):
                    cc = c if d == 0 else CHUNKS_PER_TILE - 1 - c
                    for t in range(TILES):
                        rows = pl.ds(pl.multiple_of(t * TILE + cc * CHUNK, CHUNK), CHUNK)
                        q_in = b_scr[d, rows, :]
                        s_loc = sloc_scr[d, t]
                        o_loc = lax.dot_general(q_in.astype(BF16), s_loc.astype(BF16), NT_DIMS,
                                                preferred_element_type=F32)
                        o_scr[rows, :] += o_loc
                        u_t = lax.dot_general(vb_scr[rows, :], kout_scr[d, rows, :], TN_DIMS,
                                              preferred_element_type=F32)
                        fc = fc_scr[d, pl.ds(t * CHUNKS_PER_TILE + cc, 1), :]
                        sloc_scr[d, t] = s_loc * fc + u_t
                        gt = gt_scr[d * TILES + t:d * TILES + t + 1, :]
                        qt_scr[d, rows, :] = (q_in * gt).astype(BF16)
                        gt_scr[d * TILES + t:d * TILES + t + 1, :] = gt * fc
                return carry
            lax.fori_loop(0, CHUNKS_PER_TILE, chunk_body, 0)

        for s in range(seqs_per_group):
            for d in range(2):
                s_run = s0_ref[s, d, 0].T if has_state_in else None
                order = range(tiles_per_seq) if d == 0 else range(tiles_per_seq - 1, -1, -1)
                for tt in order:
                    t = s * tiles_per_seq + tt
                    rows = pl.ds(t * TILE, TILE)
                    if s_run is None:
                        s_run = sloc_scr[d, t]
                        continue
                    o_scr[rows, :] += lax.dot_general(qt_scr[d, rows, :], s_run.astype(BF16), NT_DIMS,
                                                      preferred_element_type=F32)
                    s_run = s_run * gt_scr[d * TILES + t:d * TILES + t + 1, :] + sloc_scr[d, t]
                if has_state_out:
                    st_scr[s, d] = s_run.T

        if has_state_out:
            @pl.when(hd < GLA_HEADS)
            def _():
                sg_ref[:, 0, :, 0] = st_scr[:, :, 0:GLA_DK, :]

            @pl.when(hd >= GLA_HEADS)
            def _():
                sh_ref[:, 0, :, 0] = st_scr[...]

        def gated(rows):
            o = _rms(o_scr[rows, :]) * hn_ref[0]
            return (o * _silu(proj_scr[rows, 4 * HEAD_W:5 * HEAD_W])).astype(BF16)

        @pl.when((hd & 1) == 0)
        def _():
            def gate_rows(rows):
                mg_scr[rows, 0:HEAD_W] = gated(rows)
            _row_loop(GROUP_ROWS, ROW_BLOCK, gate_rows, unroll=2)

        @pl.when((hd & 1) == 1)
        def _():
            def gate_rows(rows):
                mg_scr[rows, HEAD_W:2 * HEAD_W] = gated(rows)
            _row_loop(GROUP_ROWS, ROW_BLOCK, gate_rows, unroll=2)

            def out_rows(rows):
                x1_ref[rows, :] += jnp.dot(mg_scr[rows, :], wout_ref[0], preferred_element_type=F32)
            _row_loop(GROUP_ROWS, MATMUL_ROWS, out_rows)

        @pl.when(hd == N_HEADS - 1)
        def _():
            gate1 = mod_ref[0, :, 2 * D_MODEL:3 * D_MODEL]

            def res_rows(rows):
                x1_ref[rows, :] = x_ref[rows, :] + gate1 * x1_ref[rows, :]
            _row_loop(GROUP_ROWS, ROW_BLOCK, res_rows)

    return kernel


def _mixer(x, mod3, mod_row, norm1, w_heads, wup, bgl, lbl, hnorm, wout_h, s0, seqs_per_group, want_states):
    groups = x.shape[0] // GROUP_ROWS
    n_seq = groups * seqs_per_group
    has_state_in = s0 is not None
    once = pl.Buffered(1)
    x_bufs = 1 if want_states else 2
    in_specs = [
        pl.BlockSpec((GROUP_ROWS, D_MODEL), lambda g, h: (g, 0), pipeline_mode=pl.Buffered(x_bufs)),
        pl.BlockSpec((1, 1, 6 * D_MODEL), lambda g, h: (mod_row(g), 0, 0)),
        pl.BlockSpec((1, D_MODEL), lambda g, h: (0, 0)),
        pl.BlockSpec((D_MODEL, PROJ_W), lambda g, h: (0, h)),
        pl.BlockSpec((1, 2, HEAD_W, HEAD_W), lambda g, h: (h, 0, 0, 0)),
        pl.BlockSpec((1, 2, HEAD_W), lambda g, h: (h, 0, 0)),
        pl.BlockSpec((1, 4, HEAD_W), lambda g, h: (h, 0, 0)),
        pl.BlockSpec((1, 1, HEAD_W), lambda g, h: (h, 0, 0)),
        pl.BlockSpec((1, 2 * HEAD_W, D_MODEL), lambda g, h: (h // 2, 0, 0)),
    ]
    args = [x, mod3, norm1, w_heads, wup, bgl, lbl, hnorm, wout_h]
    if has_state_in:
        in_specs.append(pl.BlockSpec((seqs_per_group, 2, 1, HEAD_W, HEAD_W), lambda g, h: (g, 0, h, 0, 0)))
        args.append(s0)
    out_specs = [pl.BlockSpec((GROUP_ROWS, D_MODEL), lambda g, h: (g, 0), pipeline_mode=once)]
    out_shape = [jax.ShapeDtypeStruct(x.shape, F32)]
    if want_states:
        out_specs.append(pl.BlockSpec((seqs_per_group, 1, 2, 1, GLA_DK, HEAD_W),
                                      lambda g, h: (g, 0, 0, jnp.minimum(h, GLA_HEADS - 1), 0, 0)))
        out_specs.append(pl.BlockSpec((seqs_per_group, 1, 2, 1, HEAD_W, HEAD_W),
                                      lambda g, h: (g, 0, 0, jnp.maximum(h - GLA_HEADS, 0), 0, 0)))
        out_shape.append(jax.ShapeDtypeStruct((n_seq, 1, 2, GLA_HEADS, GLA_DK, HEAD_W), F32))
        out_shape.append(jax.ShapeDtypeStruct((n_seq, 1, 2, N_HEADS - GLA_HEADS, HEAD_W, HEAD_W), F32))
    scratch = [
        pltpu.VMEM((GROUP_ROWS, D_MODEL), BF16),
        pltpu.VMEM((GROUP_ROWS, PROJ_W), F32),
        pltpu.VMEM((2, GROUP_ROWS, HEAD_W), F32),
        pltpu.VMEM((2, GROUP_ROWS, HEAD_W), F32),
        pltpu.VMEM((2, GROUP_ROWS, HEAD_W), F32),
        pltpu.VMEM((2, GROUP_ROWS, HEAD_W), BF16),
        pltpu.VMEM((2, GROUP_ROWS, HEAD_W), BF16),
        pltpu.VMEM((GROUP_ROWS, HEAD_W), BF16),
        pltpu.VMEM((2, GROUP_ROWS // CHUNK, HEAD_W), F32),
        pltpu.VMEM((GROUP_ROWS, HEAD_W), F32),
        pltpu.VMEM((2, TILES, HEAD_W, HEAD_W), F32),
        pltpu.VMEM((2 * TILES, HEAD_W), F32),
        pltpu.VMEM((seqs_per_group, 2, HEAD_W, HEAD_W), F32),
        pltpu.VMEM((GROUP_ROWS, 2 * HEAD_W), BF16),
    ]
    state_block = (seqs_per_group, 2, HEAD_W, HEAD_W)
    blocks = [((GROUP_ROWS, D_MODEL), F32, x_bufs), ((GROUP_ROWS, D_MODEL), F32, 1),
              ((D_MODEL, PROJ_W), BF16, 2), ((2 * HEAD_W, D_MODEL), BF16, 2), ((2, HEAD_W, HEAD_W), BF16, 2),
              ((6 * D_MODEL,), F32, 2), (state_block, F32, 2 * (int(has_state_in) + 2 * int(want_states)))]
    outs = pl.pallas_call(
        _make_mixer_kernel(seqs_per_group, has_state_in, want_states),
        grid=(groups, N_HEADS),
        in_specs=in_specs,
        out_specs=out_specs,
        out_shape=out_shape,
        scratch_shapes=scratch,
        compiler_params=pltpu.CompilerParams(dimension_semantics=("arbitrary", "arbitrary"),
                                             vmem_limit_bytes=_vmem_request(blocks, scratch)),
        name="mixer_ctx" if want_states else "mixer_lat",
    )(*args)
    return outs


def _make_ffn_kernel(grid_conv, seq_len):
    width = GRID_W if grid_conv else seq_len
    stride = width + CONV_GAP
    top = stride + CONV_GAP
    dys = (-1, 0, 1) if grid_conv else (0,)
    lane_tiles = FFN_TILE // HEAD_W
    lines_per_mm = MATMUL_ROWS // width
    lines_per_block = max(ROW_BLOCK // width, 1)
    block_rows = lines_per_block * width

    def kernel(x1_ref, mod_ref, n2_ref, wg_ref, wu_ref, cwg_ref, cwu_ref, cbg_ref, cbu_ref, wd_ref, fn_ref,
               y_ref, h_scr, pg_scr, pu_scr, act_scr):
        j = pl.program_id(1)

        @pl.when(j == 0)
        def _():
            shift2 = mod_ref[0, :, 3 * D_MODEL:4 * D_MODEL]
            scale2 = mod_ref[0, :, 4 * D_MODEL:5 * D_MODEL]
            n2 = n2_ref[...]

            def norm_rows(rows):
                h = _rms(x1_ref[rows, :]) * n2
                h_scr[rows, :] = (h * (1.0 + scale2) + shift2).astype(BF16)
                y_ref[rows, :] = jnp.zeros((ROW_BLOCK, D_MODEL), F32)
            _row_loop(GROUP_ROWS, ROW_BLOCK, norm_rows)
            pg_scr[...] = jnp.zeros_like(pg_scr)
            pu_scr[...] = jnp.zeros_like(pu_scr)

        def up_rows(rows):
            h = h_scr[rows, :]
            line0 = rows.start // width
            for scr, w_ref in ((pg_scr, wg_ref), (pu_scr, wu_ref)):
                res = jnp.dot(h, w_ref[...], preferred_element_type=F32)
                for lt in range(lane_tiles):
                    for ln in range(lines_per_mm):
                        dst = pl.ds(_aligned(top + (line0 + ln) * stride, 8), width)
                        scr[lt, dst, :] = res[ln * width:(ln + 1) * width, lt * HEAD_W:(lt + 1) * HEAD_W]
        _row_loop(GROUP_ROWS, MATMUL_ROWS, up_rows)

        def conv(scr, cw_ref, cb_ref, lt, base):
            lanes = slice(lt * HEAD_W, (lt + 1) * HEAD_W)
            acc = cb_ref[:, lanes]
            for dy in dys:
                for dx in (-1, 0, 1):
                    tap = cw_ref[(dy + 1) * 3 + (dx + 1):(dy + 1) * 3 + (dx + 2), lanes]
                    acc = acc + scr[lt, pl.ds(base + dy * stride + dx, width), :] * tap
            return acc

        def act_rows(i, carry):
            for ln in range(lines_per_block):
                line = i * lines_per_block + ln
                base = top + line * stride
                out = pl.ds(pl.multiple_of(line * width, width), width)
                for lt in range(lane_tiles):
                    cg = conv(pg_scr, cwg_ref, cbg_ref, lt, base)
                    cu = conv(pu_scr, cwu_ref, cbu_ref, lt, base)
                    act_scr[out, lt * HEAD_W:(lt + 1) * HEAD_W] = (_silu(cg) * cu).astype(BF16)
            return carry
        lax.fori_loop(0, GROUP_ROWS // block_rows, act_rows, 0, unroll=2)

        def down_rows(rows):
            y_ref[rows, :] += jnp.dot(act_scr[rows, :], wd_ref[...], preferred_element_type=F32)
        _row_loop(GROUP_ROWS, MATMUL_ROWS, down_rows)

        @pl.when(j == FFN_STEPS - 1)
        def _():
            gate2 = mod_ref[0, :, 5 * D_MODEL:6 * D_MODEL]
            fn = fn_ref[...]

            def res_rows(rows):
                y_ref[rows, :] = _rms(x1_ref[rows, :] + gate2 * y_ref[rows, :]) * fn
            _row_loop(GROUP_ROWS, ROW_BLOCK, res_rows)

    return kernel


def _conv_ffn(x1, mod3, mod_row, norm2, w_up, conv_w, conv_b, w_down, final_norm, grid_conv, seq_len):
    groups = x1.shape[0] // GROUP_ROWS
    in_specs = [
        pl.BlockSpec((GROUP_ROWS, D_MODEL), lambda g, j: (g, 0)),
        pl.BlockSpec((1, 1, 6 * D_MODEL), lambda g, j: (mod_row(g), 0, 0)),
        pl.BlockSpec((1, D_MODEL), lambda g, j: (0, 0)),
        pl.BlockSpec((D_MODEL, FFN_TILE), lambda g, j: (0, j)),
        pl.BlockSpec((D_MODEL, FFN_TILE), lambda g, j: (0, FFN_STEPS + j)),
        pl.BlockSpec((9, FFN_TILE), lambda g, j: (0, j)),
        pl.BlockSpec((9, FFN_TILE), lambda g, j: (0, FFN_STEPS + j)),
        pl.BlockSpec((1, FFN_TILE), lambda g, j: (0, j)),
        pl.BlockSpec((1, FFN_TILE), lambda g, j: (0, FFN_STEPS + j)),
        pl.BlockSpec((FFN_TILE, D_MODEL), lambda g, j: (j, 0)),
        pl.BlockSpec((1, D_MODEL), lambda g, j: (0, 0)),
    ]
    lane_tiles = FFN_TILE // HEAD_W
    width = GRID_W if grid_conv else seq_len
    pad_rows = (GROUP_ROWS // width + 2) * (width + CONV_GAP) + 2 * CONV_GAP
    scratch = [
        pltpu.VMEM((GROUP_ROWS, D_MODEL), BF16),
        pltpu.VMEM((lane_tiles, pad_rows, HEAD_W), F32),
        pltpu.VMEM((lane_tiles, pad_rows, HEAD_W), F32),
        pltpu.VMEM((GROUP_ROWS, FFN_TILE), BF16),
    ]
    blocks = [((GROUP_ROWS, D_MODEL), F32, 2), ((GROUP_ROWS, D_MODEL), F32, 2),
              ((D_MODEL, FFN_TILE), BF16, 4), ((FFN_TILE, D_MODEL), BF16, 2), ((16, FFN_TILE), F32, 8),
              ((6 * D_MODEL,), F32, 2)]
    return pl.pallas_call(
        _make_ffn_kernel(grid_conv, seq_len),
        grid=(groups, FFN_STEPS),
        in_specs=in_specs,
        out_specs=pl.BlockSpec((GROUP_ROWS, D_MODEL), lambda g, j: (g, 0)),
        out_shape=jax.ShapeDtypeStruct(x1.shape, F32),
        scratch_shapes=scratch,
        compiler_params=pltpu.CompilerParams(dimension_semantics=("arbitrary", "arbitrary"),
                                             vmem_limit_bytes=_vmem_request(blocks, scratch)),
        name="conv_ffn_lat" if grid_conv else "conv_ffn_ctx",
    )(x1, mod3, norm2, w_up, w_up, conv_w, conv_w, conv_b, conv_b, w_down, final_norm)


def _head_layout(w_in, w_gla_up, b_gla, hgrn_lb, gla_norm, hgrn_norm, w_out):
    gla_kw = GLA_HEADS * GLA_DK
    vw = GLA_HEADS * HEAD_W
    o_qa, o_ka, o_va, o_ga = 0, gla_kw, 2 * gla_kw, 2 * gla_kw + vw
    o_lr = o_ga + vw
    o_qb = o_lr + 2 * GLA_LOWRANK
    o_fb = o_qb + vw
    o_ib = o_fb + 2 * vw
    o_gb = o_ib + vw
    per_head = lambda off, width: w_in[:, off:off + GLA_HEADS * width].reshape(D_MODEL, GLA_HEADS, 1, width)
    low_rank = jnp.pad(w_in[:, o_lr:o_lr + 2 * GLA_LOWRANK], ((0, 0), (0, HEAD_W - 2 * GLA_LOWRANK)))
    gla = jnp.concatenate([
        jnp.concatenate([per_head(o_qa, GLA_DK), per_head(o_ka, GLA_DK)], axis=3),
        jnp.broadcast_to(low_rank[:, None, None, :], (D_MODEL, GLA_HEADS, 1, HEAD_W)),
        per_head(o_va, HEAD_W), per_head(o_ga, HEAD_W),
        jnp.zeros((D_MODEL, GLA_HEADS, 1, HEAD_W), w_in.dtype)], axis=2)
    hgrn = jnp.concatenate([per_head(o_qb, HEAD_W), per_head(o_fb, HEAD_W), per_head(o_fb + vw, HEAD_W),
                            per_head(o_ib, HEAD_W), per_head(o_gb, HEAD_W)], axis=2)
    w_heads = jnp.concatenate([gla, hgrn], axis=1).reshape(D_MODEL, N_HEADS * PROJ_W).astype(BF16)

    up = w_gla_up.astype(BF16).reshape(2, GLA_LOWRANK, GLA_HEADS, GLA_DK).transpose(2, 0, 1, 3)
    wup = jnp.stack([jnp.pad(up[:, d], ((0, N_HEADS - GLA_HEADS), (d * GLA_LOWRANK, HEAD_W - (d + 1) * GLA_LOWRANK),
                                         (0, HEAD_W - GLA_DK))) for d in range(2)], axis=1)
    bgl = jnp.pad(b_gla.reshape(2, GLA_HEADS, GLA_DK).transpose(1, 0, 2),
                  ((0, N_HEADS - GLA_HEADS), (0, 0), (0, HEAD_W - GLA_DK)))
    lb = hgrn_lb.astype(F32).reshape(2, 2, N_HEADS - GLA_HEADS, HEAD_W)
    lbl = jnp.concatenate([jnp.zeros((GLA_HEADS, 4, HEAD_W), F32),
                           lb.transpose(2, 0, 1, 3).reshape(N_HEADS - GLA_HEADS, 4, HEAD_W)], axis=0)
    hnorm = jnp.concatenate([jnp.broadcast_to(gla_norm[None, None, :], (GLA_HEADS, 1, HEAD_W)),
                             jnp.broadcast_to(hgrn_norm[None, None, :], (N_HEADS - GLA_HEADS, 1, HEAD_W))], axis=0)
    wout_h = w_out.astype(BF16).reshape(N_HEADS // 2, 2 * HEAD_W, D_MODEL)
    return w_heads, wup, bgl, lbl, hnorm, wout_h


def kernel(x_prompt, x_sample, state_gla, state_hgrn, c, c_ctx, w_ada, b_ada, norm1, norm2, w_in, w_gla_up, b_gla, hgrn_lb, gla_norm, hgrn_norm, w_out, w_ffn_up, ffn_conv, b_ffn_conv, w_ffn_down, final_norm):
    assert w_ada.shape[0] == 1 and hgrn_lb.shape[0] == 2, "single layer only"
    n_ctx, ctx_len, _ = x_prompt.shape
    n_lat, lat_len, _ = x_sample.shape
    assert lat_len == GROUP_ROWS and GROUP_ROWS % ctx_len == 0 and n_ctx % (GROUP_ROWS // ctx_len) == 0
    assert n_lat + 1 <= 8
    ctx_per_group = GROUP_ROWS // ctx_len

    cvecs = jnp.concatenate([c_ctx[None, :], c, jnp.zeros((8 - 1 - n_lat, D_MODEL), F32)], axis=0)
    mod3 = _modulation(cvecs, w_ada[0], b_ada).reshape(8, 1, 6 * D_MODEL)

    w_heads, wup, bgl, lbl, hnorm, wout_h = _head_layout(
        w_in[0], w_gla_up[0], b_gla[0], hgrn_lb, gla_norm[0], hgrn_norm[0], w_out[0])
    s0 = jnp.concatenate([jnp.pad(state_gla[:, 0], ((0, 0), (0, 0), (0, 0), (0, HEAD_W - GLA_DK), (0, 0))),
                          state_hgrn[:, 0]], axis=2)
    w_up = w_ffn_up[0].astype(BF16)
    w_down = w_ffn_down[0].astype(BF16)
    conv_w = ffn_conv[0].reshape(9, 2 * FFN_HIDDEN)
    conv_b = b_ffn_conv
    ctx_row = lambda g: 0
    lat_row = lambda g: g + 1

    xp = x_prompt.reshape(n_ctx * ctx_len, D_MODEL)
    xs = x_sample.reshape(n_lat * lat_len, D_MODEL)
    xp1, new_gla, new_hgrn = _mixer(xp, mod3, ctx_row, norm1, w_heads, wup, bgl, lbl, hnorm, wout_h, None,
                                    ctx_per_group, True)
    (xs1,) = _mixer(xs, mod3, lat_row, norm1, w_heads, wup, bgl, lbl, hnorm, wout_h, s0, 1, False)
    yp = _conv_ffn(xp1, mod3, ctx_row, norm2, w_up, conv_w, conv_b, w_down, final_norm[None, :], False, ctx_len)
    ys = _conv_ffn(xs1, mod3, lat_row, norm2, w_up, conv_w, conv_b, w_down, final_norm[None, :], True, lat_len)

    return (yp.reshape(x_prompt.shape), ys.reshape(x_sample.shape), new_gla, new_hgrn)
```

```python
import jax
import jax.numpy as jnp
from jax import lax
from jax.experimental import pallas as pl
from jax.experimental.pallas import tpu as pltpu

F32 = jnp.float32
BF16 = jnp.bfloat16

D_MODEL = 1024
N_HEADS = 8
GLA_HEADS = 4
GLA_DK = 64
HEAD_W = 128
GLA_LOWRANK = 16
GLA_GATE_NORM = 16.0
FFN_HIDDEN = 2816
GRID_W = 64
EPS = 1e-6

GROUP_ROWS = 2048
TILE = 128
CHUNK = 16
CHUNKS_PER_TILE = TILE // CHUNK
TILE_DECAY_LIMIT = 150.0
TILES = GROUP_ROWS // TILE
ROW_BLOCK = 256
MATMUL_ROWS = 1024
PROJ_W = 5 * HEAD_W
FFN_TILE = 256
FFN_STEPS = FFN_HIDDEN // FFN_TILE
CONV_GAP = 8
VMEM_SPARE = 6 * 1024 * 1024

NT_DIMS = (((1,), (1,)), ((), ()))
TN_DIMS = (((0,), (0,)), ((), ()))


def _aligned(x, m):
    return x if isinstance(x, int) else pl.multiple_of(x, m)


def _vmem_request(blocks, scratch):
    def nbytes(shape, dtype):
        n = jnp.dtype(dtype).itemsize
        for d in shape:
            n *= d
        return n
    total = sum(nbytes(shape, dtype) * bufs for shape, dtype, bufs in blocks)
    total += sum(nbytes(sc.shape, sc.dtype) for sc in scratch)
    return total + VMEM_SPARE


def _row_loop(n_rows, block, body, unroll=1):
    def step(i, carry):
        body(pl.ds(pl.multiple_of(i * block, block), block))
        return carry
    lax.fori_loop(0, n_rows // block, step, 0, unroll=unroll)


def _rms(x):
    return x * lax.rsqrt(jnp.mean(x * x, axis=-1, keepdims=True) + EPS)


def _silu(x):
    return x / (1.0 + jnp.exp(-x))


def _mod_kernel(c_ref, w_ref, b_ref, o_ref):
    a = _silu(c_ref[...]).astype(BF16)
    o_ref[...] = jnp.dot(a, w_ref[...].astype(BF16), preferred_element_type=F32) + b_ref[...]


def _modulation(cvecs, w_ada, b_ada):
    n = w_ada.shape[1]
    tn = 1024
    return pl.pallas_call(
        _mod_kernel,
        grid=(n // tn,),
        in_specs=[pl.BlockSpec((8, D_MODEL), lambda j: (0, 0)),
                  pl.BlockSpec((D_MODEL, tn), lambda j: (0, j)),
                  pl.BlockSpec((1, tn), lambda j: (0, j))],
        out_specs=pl.BlockSpec((8, tn), lambda j: (0, j)),
        out_shape=jax.ShapeDtypeStruct((8, n), F32),
        name="modulation",
    )(cvecs, w_ada, b_ada)


def _chunk_phase(q, k, v_bf, g, rev):
    row = lax.broadcasted_iota(jnp.int32, (TILE, HEAD_W), 0)
    col = lax.broadcasted_iota(jnp.int32, (TILE, HEAD_W), 1)
    cpos = row & (CHUNK - 1)
    f = jnp.exp(g)
    if rev:
        fz = jnp.where(cpos == CHUNK - 1, 0.0, f)
        diff = row - col
        edge_row = 0
    else:
        fz = jnp.where(cpos == 0, 0.0, f)
        diff = col - row
        edge_row = CHUNK - 1
    b = g
    for s in (1, 2, 4, 8):
        if rev:
            b = b + jnp.where(cpos <= CHUNK - 1 - s, pltpu.roll(b, TILE - s, 0), 0.0)
        else:
            b = b + jnp.where(cpos >= s, pltpu.roll(b, s, 0), 0.0)
    tot_rows = [b[j * CHUNK + edge_row:j * CHUNK + edge_row + 1, :] for j in range(CHUNKS_PER_TILE)]
    b_tot = jnp.concatenate([jnp.broadcast_to(r, (CHUNK, HEAD_W)) for r in tot_rows], axis=0)
    fc = jnp.exp(jnp.concatenate(tot_rows, axis=0))
    q_in = q * jnp.exp(b)
    k_out = (k * jnp.exp(b_tot - b)).astype(BF16)
    w = k
    a = jnp.zeros((TILE, TILE), F32)
    shift = TILE - 1 if rev else 1
    for d in range(CHUNK):
        if d:
            w = pltpu.roll(w, shift, 0) * fz
        s = jnp.sum(q * w, axis=1, keepdims=True)
        a = jnp.where(diff == -d, s, a)
    o_intra = jnp.dot(a.astype(BF16), v_bf, preferred_element_type=F32)
    return o_intra, q_in, k_out, fc


def _tile_cumsum(g, rev):
    row = lax.broadcasted_iota(jnp.int32, (TILE, TILE), 0)
    col = lax.broadcasted_iota(jnp.int32, (TILE, TILE), 1)
    tri = jnp.where((col >= row) if rev else (col <= row), 1.0, 0.0).astype(BF16)
    hi = g.astype(BF16)
    rest = g - hi.astype(F32)
    mid = rest.astype(BF16)
    lo = (rest - mid.astype(F32)).astype(BF16)
    parts = jnp.dot(tri, jnp.concatenate([hi, mid, lo], axis=1), preferred_element_type=F32)
    return parts[:, 0:HEAD_W] + parts[:, HEAD_W:2 * HEAD_W] + parts[:, 2 * HEAD_W:3 * HEAD_W]


def _tile_phase(q, k, v_bf, b, rev):
    row = lax.broadcasted_iota(jnp.int32, (TILE, TILE), 0)
    col = lax.broadcasted_iota(jnp.int32, (TILE, TILE), 1)
    tot = b[0:1, :] if rev else b[TILE - 1:TILE, :]
    half = 0.5 * tot
    e_half = jnp.exp(half)
    q_h = q * jnp.exp(b - half)
    k_h = k * jnp.exp(half - b)
    a = lax.dot_general(q_h.astype(BF16), k_h.astype(BF16), NT_DIMS, preferred_element_type=F32)
    a = jnp.where((col >= row) if rev else (col <= row), a, 0.0).astype(BF16)
    o_intra = jnp.dot(a, v_bf, preferred_element_type=F32)
    q_t = (q_h * e_half).astype(BF16)
    k_out = (k_h * e_half).astype(BF16)
    s_loc = lax.dot_general(v_bf, k_out, TN_DIMS, preferred_element_type=F32)
    return o_intra, q_t, s_loc, jnp.exp(tot)


def _make_mixer_kernel(seqs_per_group, has_state_in, has_state_out):
    tiles_per_seq = TILES // seqs_per_group

    def kernel(*refs):
        it = iter(refs)
        x_ref = next(it); mod_ref = next(it); n1_ref = next(it); w_ref = next(it)
        wup_ref = next(it); bg_ref = next(it); lb_ref = next(it); hn_ref = next(it); wout_ref = next(it)
        s0_ref = next(it) if has_state_in else None
        x1_ref = next(it)
        sg_ref = next(it) if has_state_out else None
        sh_ref = next(it) if has_state_out else None
        (h_scr, proj_scr, g_scr, k_scr, b_scr, kout_scr, qt_scr, vb_scr, fc_scr, o_scr,
         sloc_scr, gt_scr, st_scr, mg_scr) = it

        hd = pl.program_id(1)

        @pl.when(hd == 0)
        def _():
            shift1 = mod_ref[0, :, 0:D_MODEL]
            scale1 = mod_ref[0, :, D_MODEL:2 * D_MODEL]
            n1 = n1_ref[...]

            def norm_rows(rows):
                h = _rms(x_ref[rows, :]) * n1
                h_scr[rows, :] = (h * (1.0 + scale1) + shift1).astype(BF16)
                x1_ref[rows, :] = jnp.zeros((ROW_BLOCK, D_MODEL), F32)
            _row_loop(GROUP_ROWS, ROW_BLOCK, norm_rows, unroll=2)

        @pl.when(hd < GLA_HEADS)
        def _():
            def proj_rows(rows):
                h = h_scr[rows, :]
                proj_scr[rows, 0:2 * HEAD_W] = jnp.dot(h, w_ref[:, 0:2 * HEAD_W], preferred_element_type=F32)
                proj_scr[rows, 3 * HEAD_W:5 * HEAD_W] = jnp.dot(h, w_ref[:, 2 * HEAD_W:4 * HEAD_W],
                                                                preferred_element_type=F32)
            _row_loop(GROUP_ROWS, MATMUL_ROWS, proj_rows)

            def gla_rows(rows):
                qk = proj_scr[rows, 0:HEAD_W]
                low = lax.broadcasted_iota(jnp.int32, (ROW_BLOCK, HEAD_W), 1) < GLA_DK
                proj_scr[rows, 0:HEAD_W] = jnp.where(low, qk * (GLA_DK ** -0.5), 0.0)
                k = jnp.where(low, pltpu.roll(qk, HEAD_W - GLA_DK, 1), 0.0)
                lr = proj_scr[rows, HEAD_W:2 * HEAD_W].astype(BF16)
                for d in range(2):
                    z = jnp.dot(lr, wup_ref[0, d], preferred_element_type=F32) + bg_ref[0, d:d + 1, :]
                    log_sig = jnp.minimum(z, 0.0) - jnp.log(1.0 + jnp.exp(-jnp.abs(z)))
                    g_scr[d, rows, :] = log_sig * (1.0 / GLA_GATE_NORM)
                    k_scr[d, rows, :] = k
            _row_loop(GROUP_ROWS, ROW_BLOCK, gla_rows, unroll=2)

        @pl.when(hd >= GLA_HEADS)
        def _():
            def proj_rows(rows):
                proj_scr[rows, :] = jnp.dot(h_scr[rows, :], w_ref[...], preferred_element_type=F32)
            _row_loop(GROUP_ROWS, MATMUL_ROWS, proj_rows)

            def hgrn_rows(rows):
                for d in range(2):
                    a0 = lb_ref[0, d:d + 1, :]
                    a1 = lb_ref[0, 2 + d:3 + d, :]
                    m = jnp.maximum(a0, a1)
                    e0 = jnp.exp(a0 - m)
                    e1 = jnp.exp(a1 - m)
                    lb = e0 / (e0 + e1)
                    xr = proj_scr[rows, (1 + d) * HEAD_W:(2 + d) * HEAD_W]
                    t = jnp.exp(-jnp.abs(xr))
                    big = 1.0 / (1.0 + t)
                    small = t * big
                    pos = xr >= 0.0
                    g_scr[d, rows, :] = jnp.log(lb + (1.0 - lb) * jnp.where(pos, big, small))
                    k_scr[d, rows, :] = (1.0 - lb) * jnp.where(pos, small, big)
            _row_loop(GROUP_ROWS, ROW_BLOCK, hgrn_rows, unroll=2)

        def cum_body(t, lowest):
            rows = pl.ds(pl.multiple_of(t * TILE, TILE), TILE)
            for d in range(2):
                b = _tile_cumsum(g_scr[d, rows, :], rev=(d == 1))
                b_scr[d, rows, :] = b
                lowest = jnp.minimum(lowest, b[0:1, :] if d == 1 else b[TILE - 1:TILE, :])
            return lowest
        lowest = lax.fori_loop(0, TILES, cum_body, jnp.zeros((1, HEAD_W), F32), unroll=TILES)
        tile_safe = jnp.min(lowest) >= -TILE_DECAY_LIMIT

        @pl.when(tile_safe)
        def _():
            def tile_body(t, carry):
                rows = pl.ds(pl.multiple_of(t * TILE, TILE), TILE)
                q = proj_scr[rows, 0:HEAD_W]
                v_bf = proj_scr[rows, 3 * HEAD_W:4 * HEAD_W].astype(BF16)
                o_sum = None
                for d in range(2):
                    o_intra, q_t, s_loc, g_t = _tile_phase(q, k_scr[d, rows, :], v_bf, b_scr[d, rows, :], rev=(d == 1))
                    qt_scr[d, rows, :] = q_t
                    sloc_scr[d, t] = s_loc
                    gt_scr[pl.ds(d * TILES + t, 1), :] = g_t
                    o_sum = o_intra if o_sum is None else o_sum + o_intra
                o_scr[rows, :] = o_sum
                return carry
            lax.fori_loop(0, TILES, tile_body, 0, unroll=TILES)

        @pl.when(jnp.logical_not(tile_safe))
        def _():
            def tile_body(t, carry):
                rows = pl.ds(pl.multiple_of(t * TILE, TILE), TILE)
                q = proj_scr[rows, 0:HEAD_W]
                v_bf = proj_scr[rows, 3 * HEAD_W:4 * HEAD_W].astype(BF16)
                vb_scr[rows, :] = v_bf
                o_sum = None
                for d in range(2):
                    o_intra, q_in, k_out, fc = _chunk_phase(q, k_scr[d, rows, :], v_bf, g_scr[d, rows, :], rev=(d == 1))
                    b_scr[d, rows, :] = q_in
                    kout_scr[d, rows, :] = k_out
                    fc_scr[d, pl.ds(pl.multiple_of(t * CHUNKS_PER_TILE, CHUNKS_PER_TILE), CHUNKS_PER_TILE), :] = fc
                    o_sum = o_intra if o_sum is None else o_sum + o_intra
                o_scr[rows, :] = o_sum
                return carry
            lax.fori_loop(0, TILES, tile_body, 0)

            sloc_scr[...] = jnp.zeros_like(sloc_scr)
            gt_scr[...] = jnp.ones_like(gt_scr)

            def chunk_body(c, carry):
                for d in range(2):
                    cc = c if d == 0 else CHUNKS_PER_TILE - 1 - c
                    for t in range(TILES):
                        rows = pl.ds(pl.multiple_of(t * TILE + cc * CHUNK, CHUNK), CHUNK)
                        q_in = b_scr[d, rows, :]
                        s_loc = sloc_scr[d, t]
                        o_loc = lax.dot_general(q_in.astype(BF16), s_loc.astype(BF16), NT_DIMS,
                                                preferred_element_type=F32)
                        o_scr[rows, :] += o_loc
                        u_t = lax.dot_general(vb_scr[rows, :], kout_scr[d, rows, :], TN_DIMS,
                                              preferred_element_type=F32)
                        fc = fc_scr[d, pl.ds(t * CHUNKS_PER_TILE + cc, 1), :]
                        sloc_scr[d, t] = s_loc * fc + u_t
                        gt = gt_scr[d * TILES + t:d * TILES + t + 1, :]
                        qt_scr[d, rows, :] = (q_in * gt).astype(BF16)
                        gt_scr[d * TILES + t:d * TILES + t + 1, :] = gt * fc
                return carry
            lax.fori_loop(0, CHUNKS_PER_TILE, chunk_body, 0)

        for s in range(seqs_per_group):
            for d in range(2):
                s_run = s0_ref[s, d, 0].T if has_state_in else None
                order = range(tiles_per_seq) if d == 0 else range(tiles_per_seq - 1, -1, -1)
                for tt in order:
                    t = s * tiles_per_seq + tt
                    rows = pl.ds(t * TILE, TILE)
                    if s_run is None:
                        s_run = sloc_scr[d, t]
                        continue
                    o_scr[rows, :] += lax.dot_general(qt_scr[d, rows, :], s_run.astype(BF16), NT_DIMS,
                                                      preferred_element_type=F32)
                    s_run = s_run * gt_scr[d * TILES + t:d * TILES + t + 1, :] + sloc_scr[d, t]
                if has_state_out:
                    st_scr[s, d] = s_run.T

        if has_state_out:
            @pl.when(hd < GLA_HEADS)
            def _():
                sg_ref[:, 0, :, 0] = st_scr[:, :, 0:GLA_DK, :]

            @pl.when(hd >= GLA_HEADS)
            def _():
                sh_ref[:, 0, :, 0] = st_scr[...]

        def gated(rows):
            o = _rms(o_scr[rows, :]) * hn_ref[0]
            return (o * _silu(proj_scr[rows, 4 * HEAD_W:5 * HEAD_W])).astype(BF16)

        @pl.when((hd & 1) == 0)
        def _():
            def gate_rows(rows):
                mg_scr[rows, 0:HEAD_W] = gated(rows)
            _row_loop(GROUP_ROWS, ROW_BLOCK, gate_rows, unroll=2)

        @pl.when((hd & 1) == 1)
        def _():
            def gate_rows(rows):
                mg_scr[rows, HEAD_W:2 * HEAD_W] = gated(rows)
            _row_loop(GROUP_ROWS, ROW_BLOCK, gate_rows, unroll=2)

            def out_rows(rows):
                x1_ref[rows, :] += jnp.dot(mg_scr[rows, :], wout_ref[0], preferred_element_type=F32)
            _row_loop(GROUP_ROWS, MATMUL_ROWS, out_rows)

        @pl.when(hd == N_HEADS - 1)
        def _():
            gate1 = mod_ref[0, :, 2 * D_MODEL:3 * D_MODEL]

            def res_rows(rows):
                x1_ref[rows, :] = x_ref[rows, :] + gate1 * x1_ref[rows, :]
            _row_loop(GROUP_ROWS, ROW_BLOCK, res_rows)

    return kernel


def _mixer(x, mod3, mod_row, norm1, w_heads, wup, bgl, lbl, hnorm, wout_h, s0, seqs_per_group, want_states):
    groups = x.shape[0] // GROUP_ROWS
    n_seq = groups * seqs_per_group
    has_state_in = s0 is not None
    once = pl.Buffered(1)
    x_bufs = 1 if want_states else 2
    in_specs = [
        pl.BlockSpec((GROUP_ROWS, D_MODEL), lambda g, h: (g, 0), pipeline_mode=pl.Buffered(x_bufs)),
        pl.BlockSpec((1, 1, 6 * D_MODEL), lambda g, h: (mod_row(g), 0, 0)),
        pl.BlockSpec((1, D_MODEL), lambda g, h: (0, 0)),
        pl.BlockSpec((D_MODEL, PROJ_W), lambda g, h: (0, h)),
        pl.BlockSpec((1, 2, HEAD_W, HEAD_W), lambda g, h: (h, 0, 0, 0)),
        pl.BlockSpec((1, 2, HEAD_W), lambda g, h: (h, 0, 0)),
        pl.BlockSpec((1, 4, HEAD_W), lambda g, h: (h, 0, 0)),
        pl.BlockSpec((1, 1, HEAD_W), lambda g, h: (h, 0, 0)),
        pl.BlockSpec((1, 2 * HEAD_W, D_MODEL), lambda g, h: (h // 2, 0, 0)),
    ]
    args = [x, mod3, norm1, w_heads, wup, bgl, lbl, hnorm, wout_h]
    if has_state_in:
        in_specs.append(pl.BlockSpec((seqs_per_group, 2, 1, HEAD_W, HEAD_W), lambda g, h: (g, 0, h, 0, 0)))
        args.append(s0)
    out_specs = [pl.BlockSpec((GROUP_ROWS, D_MODEL), lambda g, h: (g, 0), pipeline_mode=once)]
    out_shape = [jax.ShapeDtypeStruct(x.shape, F32)]
    if want_states:
        out_specs.append(pl.BlockSpec((seqs_per_group, 1, 2, 1, GLA_DK, HEAD_W),
                                      lambda g, h: (g, 0, 0, jnp.minimum(h, GLA_HEADS - 1), 0, 0)))
        out_specs.append(pl.BlockSpec((seqs_per_group, 1, 2, 1, HEAD_W, HEAD_W),
                                      lambda g, h: (g, 0, 0, jnp.maximum(h - GLA_HEADS, 0), 0, 0)))
        out_shape.append(jax.ShapeDtypeStruct((n_seq, 1, 2, GLA_HEADS, GLA_DK, HEAD_W), F32))
        out_shape.append(jax.ShapeDtypeStruct((n_seq, 1, 2, N_HEADS - GLA_HEADS, HEAD_W, HEAD_W), F32))
    scratch = [
        pltpu.VMEM((GROUP_ROWS, D_MODEL), BF16),
        pltpu.VMEM((GROUP_ROWS, PROJ_W), F32),
        pltpu.VMEM((2, GROUP_ROWS, HEAD_W), F32),
        pltpu.VMEM((2, GROUP_ROWS, HEAD_W), F32),
        pltpu.VMEM((2, GROUP_ROWS, HEAD_W), F32),
        pltpu.VMEM((2, GROUP_ROWS, HEAD_W), BF16),
        pltpu.VMEM((2, GROUP_ROWS, HEAD_W), BF16),
        pltpu.VMEM((GROUP_ROWS, HEAD_W), BF16),
        pltpu.VMEM((2, GROUP_ROWS // CHUNK, HEAD_W), F32),
        pltpu.VMEM((GROUP_ROWS, HEAD_W), F32),
        pltpu.VMEM((2, TILES, HEAD_W, HEAD_W), F32),
        pltpu.VMEM((2 * TILES, HEAD_W), F32),
        pltpu.VMEM((seqs_per_group, 2, HEAD_W, HEAD_W), F32),
        pltpu.VMEM((GROUP_ROWS, 2 * HEAD_W), BF16),
    ]
    state_block = (seqs_per_group, 2, HEAD_W, HEAD_W)
    blocks = [((GROUP_ROWS, D_MODEL), F32, x_bufs), ((GROUP_ROWS, D_MODEL), F32, 1),
              ((D_MODEL, PROJ_W), BF16, 2), ((2 * HEAD_W, D_MODEL), BF16, 2), ((2, HEAD_W, HEAD_W), BF16, 2),
              ((6 * D_MODEL,), F32, 2), (state_block, F32, 2 * (int(has_state_in) + 2 * int(want_states)))]
    outs = pl.pallas_call(
        _make_mixer_kernel(seqs_per_group, has_state_in, want_states),
        grid=(groups, N_HEADS),
        in_specs=in_specs,
        out_specs=out_specs,
        out_shape=out_shape,
        scratch_shapes=scratch,
        compiler_params=pltpu.CompilerParams(dimension_semantics=("arbitrary", "arbitrary"),
                                             vmem_limit_bytes=_vmem_request(blocks, scratch)),
        name="mixer_ctx" if want_states else "mixer_lat",
    )(*args)
    return outs


def _make_ffn_kernel(grid_conv, seq_len):
    width = GRID_W if grid_conv else seq_len
    stride = width + CONV_GAP
    top = stride + CONV_GAP
    dys = (-1, 0, 1) if grid_conv else (0,)
    lane_tiles = FFN_TILE // HEAD_W
    lines_per_mm = MATMUL_ROWS // width
    lines_per_block = max(ROW_BLOCK // width, 1)
    block_rows = lines_per_block * width

    def kernel(x1_ref, mod_ref, n2_ref, wg_ref, wu_ref, cwg_ref, cwu_ref, cbg_ref, cbu_ref, wd_ref, fn_ref,
               y_ref, h_scr, pg_scr, pu_scr, act_scr):
        j = pl.program_id(1)

        @pl.when(j == 0)
        def _():
            shift2 = mod_ref[0, :, 3 * D_MODEL:4 * D_MODEL]
            scale2 = mod_ref[0, :, 4 * D_MODEL:5 * D_MODEL]
            n2 = n2_ref[...]

            def norm_rows(rows):
                h = _rms(x1_ref[rows, :]) * n2
                h_scr[rows, :] = (h * (1.0 + scale2) + shift2).astype(BF16)
                y_ref[rows, :] = jnp.zeros((ROW_BLOCK, D_MODEL), F32)
            _row_loop(GROUP_ROWS, ROW_BLOCK, norm_rows, unroll=2)
            pg_scr[...] = jnp.zeros_like(pg_scr)
            pu_scr[...] = jnp.zeros_like(pu_scr)

        def up_rows(rows):
            h = h_scr[rows, :]
            line0 = rows.start // width
            for scr, w_ref in ((pg_scr, wg_ref), (pu_scr, wu_ref)):
                res = jnp.dot(h, w_ref[...], preferred_element_type=F32)
                for lt in range(lane_tiles):
                    for ln in range(lines_per_mm):
                        dst = pl.ds(_aligned(top + (line0 + ln) * stride, 8), width)
                        scr[lt, dst, :] = res[ln * width:(ln + 1) * width, lt * HEAD_W:(lt + 1) * HEAD_W]
        _row_loop(GROUP_ROWS, MATMUL_ROWS, up_rows)

        def conv(scr, cw_ref, cb_ref, lt, base):
            lanes = slice(lt * HEAD_W, (lt + 1) * HEAD_W)
            acc = cb_ref[:, lanes]
            for dy in dys:
                for dx in (-1, 0, 1):
                    tap = cw_ref[(dy + 1) * 3 + (dx + 1):(dy + 1) * 3 + (dx + 2), lanes]
                    acc = acc + scr[lt, pl.ds(base + dy * stride + dx, width), :] * tap
            return acc

        def act_rows(i, carry):
            for ln in range(lines_per_block):
                line = i * lines_per_block + ln
                base = top + line * stride
                out = pl.ds(pl.multiple_of(line * width, width), width)
                for lt in range(lane_tiles):
                    cg = conv(pg_scr, cwg_ref, cbg_ref, lt, base)
                    cu = conv(pu_scr, cwu_ref, cbu_ref, lt, base)
                    act_scr[out, lt * HEAD_W:(lt + 1) * HEAD_W] = (_silu(cg) * cu).astype(BF16)
            return carry
        lax.fori_loop(0, GROUP_ROWS // block_rows, act_rows, 0, unroll=2)

        def down_rows(rows):
            y_ref[rows, :] += jnp.dot(act_scr[rows, :], wd_ref[...], preferred_element_type=F32)
        _row_loop(GROUP_ROWS, MATMUL_ROWS, down_rows)

        @pl.when(j == FFN_STEPS - 1)
        def _():
            gate2 = mod_ref[0, :, 5 * D_MODEL:6 * D_MODEL]
            fn = fn_ref[...]

            def res_rows(rows):
                y_ref[rows, :] = _rms(x1_ref[rows, :] + gate2 * y_ref[rows, :]) * fn
            _row_loop(GROUP_ROWS, ROW_BLOCK, res_rows)

    return kernel


def _conv_ffn(x1, mod3, mod_row, norm2, w_up, conv_w, conv_b, w_down, final_norm, grid_conv, seq_len):
    groups = x1.shape[0] // GROUP_ROWS
    in_specs = [
        pl.BlockSpec((GROUP_ROWS, D_MODEL), lambda g, j: (g, 0)),
        pl.BlockSpec((1, 1, 6 * D_MODEL), lambda g, j: (mod_row(g), 0, 0)),
        pl.BlockSpec((1, D_MODEL), lambda g, j: (0, 0)),
        pl.BlockSpec((D_MODEL, FFN_TILE), lambda g, j: (0, j)),
        pl.BlockSpec((D_MODEL, FFN_TILE), lambda g, j: (0, FFN_STEPS + j)),
        pl.BlockSpec((9, FFN_TILE), lambda g, j: (0, j)),
        pl.BlockSpec((9, FFN_TILE), lambda g, j: (0, FFN_STEPS + j)),
        pl.BlockSpec((1, FFN_TILE), lambda g, j: (0, j)),
        pl.BlockSpec((1, FFN_TILE), lambda g, j: (0, FFN_STEPS + j)),
        pl.BlockSpec((FFN_TILE, D_MODEL), lambda g, j: (j, 0)),
        pl.BlockSpec((1, D_MODEL), lambda g, j: (0, 0)),
    ]
    lane_tiles = FFN_TILE // HEAD_W
    width = GRID_W if grid_conv else seq_len
    pad_rows = (GROUP_ROWS // width + 2) * (width + CONV_GAP) + 2 * CONV_GAP
    scratch = [
        pltpu.VMEM((GROUP_ROWS, D_MODEL), BF16),
        pltpu.VMEM((lane_tiles, pad_rows, HEAD_W), F32),
        pltpu.VMEM((lane_tiles, pad_rows, HEAD_W), F32),
        pltpu.VMEM((GROUP_ROWS, FFN_TILE), BF16),
    ]
    blocks = [((GROUP_ROWS, D_MODEL), F32, 2), ((GROUP_ROWS, D_MODEL), F32, 2),
              ((D_MODEL, FFN_TILE), BF16, 4), ((FFN_TILE, D_MODEL), BF16, 2), ((16, FFN_TILE), F32, 8),
              ((6 * D_MODEL,), F32, 2)]
    return pl.pallas_call(
        _make_ffn_kernel(grid_conv, seq_len),
        grid=(groups, FFN_STEPS),
        in_specs=in_specs,
        out_specs=pl.BlockSpec((GROUP_ROWS, D_MODEL), lambda g, j: (g, 0)),
        out_shape=jax.ShapeDtypeStruct(x1.shape, F32),
        scratch_shapes=scratch,
        compiler_params=pltpu.CompilerParams(dimension_semantics=("arbitrary", "arbitrary"),
                                             vmem_limit_bytes=_vmem_request(blocks, scratch)),
        name="conv_ffn_lat" if grid_conv else "conv_ffn_ctx",
    )(x1, mod3, norm2, w_up, w_up, conv_w, conv_w, conv_b, conv_b, w_down, final_norm)


def _head_layout(w_in, w_gla_up, b_gla, hgrn_lb, gla_norm, hgrn_norm, w_out):
    gla_kw = GLA_HEADS * GLA_DK
    vw = GLA_HEADS * HEAD_W
    o_qa, o_ka, o_va, o_ga = 0, gla_kw, 2 * gla_kw, 2 * gla_kw + vw
    o_lr = o_ga + vw
    o_qb = o_lr + 2 * GLA_LOWRANK
    o_fb = o_qb + vw
    o_ib = o_fb + 2 * vw
    o_gb = o_ib + vw
    per_head = lambda off, width: w_in[:, off:off + GLA_HEADS * width].reshape(D_MODEL, GLA_HEADS, 1, width)
    low_rank = jnp.pad(w_in[:, o_lr:o_lr + 2 * GLA_LOWRANK], ((0, 0), (0, HEAD_W - 2 * GLA_LOWRANK)))
    gla = jnp.concatenate([
        jnp.concatenate([per_head(o_qa, GLA_DK), per_head(o_ka, GLA_DK)], axis=3),
        jnp.broadcast_to(low_rank[:, None, None, :], (D_MODEL, GLA_HEADS, 1, HEAD_W)),
        per_head(o_va, HEAD_W), per_head(o_ga, HEAD_W),
        jnp.zeros((D_MODEL, GLA_HEADS, 1, HEAD_W), w_in.dtype)], axis=2)
    hgrn = jnp.concatenate([per_head(o_qb, HEAD_W), per_head(o_fb, HEAD_W), per_head(o_fb + vw, HEAD_W),
                            per_head(o_ib, HEAD_W), per_head(o_gb, HEAD_W)], axis=2)
    w_heads = jnp.concatenate([gla, hgrn], axis=1).reshape(D_MODEL, N_HEADS * PROJ_W).astype(BF16)

    up = w_gla_up.astype(BF16).reshape(2, GLA_LOWRANK, GLA_HEADS, GLA_DK).transpose(2, 0, 1, 3)
    wup = jnp.stack([jnp.pad(up[:, d], ((0, N_HEADS - GLA_HEADS), (d * GLA_LOWRANK, HEAD_W - (d + 1) * GLA_LOWRANK),
                                         (0, HEAD_W - GLA_DK))) for d in range(2)], axis=1)
    bgl = jnp.pad(b_gla.reshape(2, GLA_HEADS, GLA_DK).transpose(1, 0, 2),
                  ((0, N_HEADS - GLA_HEADS), (0, 0), (0, HEAD_W - GLA_DK)))
    lb = hgrn_lb.astype(F32).reshape(2, 2, N_HEADS - GLA_HEADS, HEAD_W)
    lbl = jnp.concatenate([jnp.zeros((GLA_HEADS, 4, HEAD_W), F32),
                           lb.transpose(2, 0, 1, 3).reshape(N_HEADS - GLA_HEADS, 4, HEAD_W)], axis=0)
    hnorm = jnp.concatenate([jnp.broadcast_to(gla_norm[None, None, :], (GLA_HEADS, 1, HEAD_W)),
                             jnp.broadcast_to(hgrn_norm[None, None, :], (N_HEADS - GLA_HEADS, 1, HEAD_W))], axis=0)
    wout_h = w_out.astype(BF16).reshape(N_HEADS // 2, 2 * HEAD_W, D_MODEL)
    return w_heads, wup, bgl, lbl, hnorm, wout_h


def kernel(x_prompt, x_sample, state_gla, state_hgrn, c, c_ctx, w_ada, b_ada, norm1, norm2, w_in, w_gla_up, b_gla, hgrn_lb, gla_norm, hgrn_norm, w_out, w_ffn_up, ffn_conv, b_ffn_conv, w_ffn_down, final_norm):
    assert w_ada.shape[0] == 1 and hgrn_lb.shape[0] == 2, "single layer only"
    n_ctx, ctx_len, _ = x_prompt.shape
    n_lat, lat_len, _ = x_sample.shape
    assert lat_len == GROUP_ROWS and GROUP_ROWS % ctx_len == 0 and n_ctx % (GROUP_ROWS // ctx_len) == 0
    assert n_lat + 1 <= 8
    ctx_per_group = GROUP_ROWS // ctx_len

    cvecs = jnp.concatenate([c_ctx[None, :], c, jnp.zeros((8 - 1 - n_lat, D_MODEL), F32)], axis=0)
    mod3 = _modulation(cvecs, w_ada[0], b_ada).reshape(8, 1, 6 * D_MODEL)

    w_heads, wup, bgl, lbl, hnorm, wout_h = _head_layout(
        w_in[0], w_gla_up[0], b_gla[0], hgrn_lb, gla_norm[0], hgrn_norm[0], w_out[0])
    s0 = jnp.concatenate([jnp.pad(state_gla[:, 0], ((0, 0), (0, 0), (0, 0), (0, HEAD_W - GLA_DK), (0, 0))),
                          state_hgrn[:, 0]], axis=2)
    w_up = w_ffn_up[0].astype(BF16)
    w_down = w_ffn_down[0].astype(BF16)
    conv_w = ffn_conv[0].reshape(9, 2 * FFN_HIDDEN)
    conv_b = b_ffn_conv
    ctx_row = lambda g: 0
    lat_row = lambda g: g + 1

    xp = x_prompt.reshape(n_ctx * ctx_len, D_MODEL)
    xs = x_sample.reshape(n_lat * lat_len, D_MODEL)
    xp1, new_gla, new_hgrn = _mixer(xp, mod3, ctx_row, norm1, w_heads, wup, bgl, lbl, hnorm, wout_h, None,
                                    ctx_per_group, True)
    (xs1,) = _mixer(xs, mod3, lat_row, norm1, w_heads, wup, bgl, lbl, hnorm, wout_h, s0, 1, False)
    yp = _conv_ffn(xp1, mod3, ctx_row, norm2, w_up, conv_w, conv_b, w_down, final_norm[None, :], False, ctx_len)
    ys = _conv_ffn(xs1, mod3, lat_row, norm2, w_up, conv_w, conv_b, w_down, final_norm[None, :], True, lat_len)

    return (yp.reshape(x_prompt.shape), ys.reshape(x_sample.shape), new_gla, new_hgrn)
```

```python
import jax
import jax.numpy as jnp
from jax import lax
from jax.experimental import pallas as pl
from jax.experimental.pallas import tpu as pltpu

F32 = jnp.float32
BF16 = jnp.bfloat16

D_MODEL = 1024
N_HEADS = 8
GLA_HEADS = 4
GLA_DK = 64
HEAD_W = 128
GLA_LOWRANK = 16
GLA_GATE_NORM = 16.0
FFN_HIDDEN = 2816
GRID_W = 64
EPS = 1e-6

GROUP_ROWS = 2048
TILE = 128
CHUNK = 16
CHUNKS_PER_TILE = TILE // CHUNK
TILE_DECAY_LIMIT = 150.0
TILES = GROUP_ROWS // TILE
ROW_BLOCK = 256
MATMUL_ROWS = 1024
PROJ_W = 5 * HEAD_W
FFN_TILE = 256
FFN_STEPS = FFN_HIDDEN // FFN_TILE
CONV_GAP = 8
VMEM_SPARE = 6 * 1024 * 1024
VMEM_CAP = 58 * 1024 * 1024

NT_DIMS = (((1,), (1,)), ((), ()))
TN_DIMS = (((0,), (0,)), ((), ()))


def _aligned(x, m):
    return x if isinstance(x, int) else pl.multiple_of(x, m)


def _vmem_request(blocks, scratch):
    def nbytes(shape, dtype):
        n = jnp.dtype(dtype).itemsize
        for d in shape:
            n *= d
        return n
    total = sum(nbytes(shape, dtype) * bufs for shape, dtype, bufs in blocks)
    total += sum(nbytes(sc.shape, sc.dtype) for sc in scratch)
    return min(total + VMEM_SPARE, VMEM_CAP)


def _row_loop(n_rows, block, body, unroll=1):
    def step(i, carry):
        body(pl.ds(pl.multiple_of(i * block, block), block))
        return carry
    lax.fori_loop(0, n_rows // block, step, 0, unroll=unroll)


def _rms(x):
    return x * lax.rsqrt(jnp.mean(x * x, axis=-1, keepdims=True) + EPS)


def _silu(x):
    return x / (1.0 + jnp.exp(-x))


def _mod_kernel(c_ref, w_ref, b_ref, o_ref):
    a = _silu(c_ref[...]).astype(BF16)
    o_ref[...] = jnp.dot(a, w_ref[...].astype(BF16), preferred_element_type=F32) + b_ref[...]


def _modulation(cvecs, w_ada, b_ada):
    n = w_ada.shape[1]
    tn = 1024
    return pl.pallas_call(
        _mod_kernel,
        grid=(n // tn,),
        in_specs=[pl.BlockSpec((8, D_MODEL), lambda j: (0, 0)),
                  pl.BlockSpec((D_MODEL, tn), lambda j: (0, j)),
                  pl.BlockSpec((1, tn), lambda j: (0, j))],
        out_specs=pl.BlockSpec((8, tn), lambda j: (0, j)),
        out_shape=jax.ShapeDtypeStruct((8, n), F32),
        name="modulation",
    )(cvecs, w_ada, b_ada)


def _chunk_phase(q, k, v_bf, g, rev):
    row = lax.broadcasted_iota(jnp.int32, (TILE, HEAD_W), 0)
    col = lax.broadcasted_iota(jnp.int32, (TILE, HEAD_W), 1)
    cpos = row & (CHUNK - 1)
    f = jnp.exp(g)
    if rev:
        fz = jnp.where(cpos == CHUNK - 1, 0.0, f)
        diff = row - col
        edge_row = 0
    else:
        fz = jnp.where(cpos == 0, 0.0, f)
        diff = col - row
        edge_row = CHUNK - 1
    b = g
    for s in (1, 2, 4, 8):
        if rev:
            b = b + jnp.where(cpos <= CHUNK - 1 - s, pltpu.roll(b, TILE - s, 0), 0.0)
        else:
            b = b + jnp.where(cpos >= s, pltpu.roll(b, s, 0), 0.0)
    tot_rows = [b[j * CHUNK + edge_row:j * CHUNK + edge_row + 1, :] for j in range(CHUNKS_PER_TILE)]
    b_tot = jnp.concatenate([jnp.broadcast_to(r, (CHUNK, HEAD_W)) for r in tot_rows], axis=0)
    fc = jnp.exp(jnp.concatenate(tot_rows, axis=0))
    q_in = q * jnp.exp(b)
    k_out = (k * jnp.exp(b_tot - b)).astype(BF16)
    w = k
    a = jnp.zeros((TILE, TILE), F32)
    shift = TILE - 1 if rev else 1
    for d in range(CHUNK):
        if d:
            w = pltpu.roll(w, shift, 0) * fz
        s = jnp.sum(q * w, axis=1, keepdims=True)
        a = jnp.where(diff == -d, s, a)
    o_intra = jnp.dot(a.astype(BF16), v_bf, preferred_element_type=F32)
    return o_intra, q_in, k_out, fc


def _tile_cumsum(g, rev):
    row = lax.broadcasted_iota(jnp.int32, (TILE, TILE), 0)
    col = lax.broadcasted_iota(jnp.int32, (TILE, TILE), 1)
    tri = jnp.where((col >= row) if rev else (col <= row), 1.0, 0.0).astype(BF16)
    hi = g.astype(BF16)
    rest = g - hi.astype(F32)
    mid = rest.astype(BF16)
    lo = (rest - mid.astype(F32)).astype(BF16)
    parts = jnp.dot(tri, jnp.concatenate([hi, mid, lo], axis=1), preferred_element_type=F32)
    return parts[:, 0:HEAD_W] + parts[:, HEAD_W:2 * HEAD_W] + parts[:, 2 * HEAD_W:3 * HEAD_W]


def _tile_phase(q, k, v_bf, b, rev):
    row = lax.broadcasted_iota(jnp.int32, (TILE, TILE), 0)
    col = lax.broadcasted_iota(jnp.int32, (TILE, TILE), 1)
    tot = b[0:1, :] if rev else b[TILE - 1:TILE, :]
    half = 0.5 * tot
    e_half = jnp.exp(half)
    q_h = q * jnp.exp(b - half)
    k_h = k * jnp.exp(half - b)
    a = lax.dot_general(q_h.astype(BF16), k_h.astype(BF16), NT_DIMS, preferred_element_type=F32)
    a = jnp.where((col >= row) if rev else (col <= row), a, 0.0).astype(BF16)
    o_intra = jnp.dot(a, v_bf, preferred_element_type=F32)
    q_t = (q_h * e_half).astype(BF16)
    k_out = (k_h * e_half).astype(BF16)
    s_loc = lax.dot_general(v_bf, k_out, TN_DIMS, preferred_element_type=F32)
    return o_intra, q_t, s_loc, jnp.exp(tot)


def _make_mixer_kernel(seqs_per_group, has_state_in, has_state_out):
    tiles_per_seq = TILES // seqs_per_group

    def kernel(*refs):
        it = iter(refs)
        x_ref = next(it); mod_ref = next(it); n1_ref = next(it); w_ref = next(it)
        wup_ref = next(it); bg_ref = next(it); lb_ref = next(it); hn_ref = next(it); wout_ref = next(it)
        s0_ref = next(it) if has_state_in else None
        x1_ref = next(it)
        sg_ref = next(it) if has_state_out else None
        sh_ref = next(it) if has_state_out else None
        (h_scr, proj_scr, g_scr, k_scr, b_scr, kout_scr, qt_scr, vb_scr, fc_scr, o_scr,
         sloc_scr, gt_scr, st_scr, mg_scr) = it

        hd = pl.program_id(1)

        @pl.when(hd == 0)
        def _():
            shift1 = mod_ref[0, :, 0:D_MODEL]
            scale1 = mod_ref[0, :, D_MODEL:2 * D_MODEL]
            n1 = n1_ref[...]

            def norm_rows(rows):
                h = _rms(x_ref[rows, :]) * n1
                h_scr[rows, :] = (h * (1.0 + scale1) + shift1).astype(BF16)
                x1_ref[rows, :] = jnp.zeros((ROW_BLOCK, D_MODEL), F32)
            _row_loop(GROUP_ROWS, ROW_BLOCK, norm_rows, unroll=2)

        @pl.when(hd < GLA_HEADS)
        def _():
            def proj_rows(rows):
                h = h_scr[rows, :]
                proj_scr[rows, 0:2 * HEAD_W] = jnp.dot(h, w_ref[:, 0:2 * HEAD_W], preferred_element_type=F32)
                proj_scr[rows, 3 * HEAD_W:5 * HEAD_W] = jnp.dot(h, w_ref[:, 2 * HEAD_W:4 * HEAD_W],
                                                                preferred_element_type=F32)
            _row_loop(GROUP_ROWS, MATMUL_ROWS, proj_rows)

            def gla_rows(rows):
                qk = proj_scr[rows, 0:HEAD_W]
                low = lax.broadcasted_iota(jnp.int32, (ROW_BLOCK, HEAD_W), 1) < GLA_DK
                proj_scr[rows, 0:HEAD_W] = jnp.where(low, qk * (GLA_DK ** -0.5), 0.0)
                k = jnp.where(low, pltpu.roll(qk, HEAD_W - GLA_DK, 1), 0.0)
                lr = proj_scr[rows, HEAD_W:2 * HEAD_W].astype(BF16)
                for d in range(2):
                    z = jnp.dot(lr, wup_ref[0, d], preferred_element_type=F32) + bg_ref[0, d:d + 1, :]
                    log_sig = jnp.minimum(z, 0.0) - jnp.log(1.0 + jnp.exp(-jnp.abs(z)))
                    g_scr[d, rows, :] = log_sig * (1.0 / GLA_GATE_NORM)
                    k_scr[d, rows, :] = k
            _row_loop(GROUP_ROWS, ROW_BLOCK, gla_rows, unroll=2)

        @pl.when(hd >= GLA_HEADS)
        def _():
            def proj_rows(rows):
                proj_scr[rows, :] = jnp.dot(h_scr[rows, :], w_ref[...], preferred_element_type=F32)
            _row_loop(GROUP_ROWS, MATMUL_ROWS, proj_rows)

            def hgrn_rows(rows):
                for d in range(2):
                    a0 = lb_ref[0, d:d + 1, :]
                    a1 = lb_ref[0, 2 + d:3 + d, :]
                    m = jnp.maximum(a0, a1)
                    e0 = jnp.exp(a0 - m)
                    e1 = jnp.exp(a1 - m)
                    lb = e0 / (e0 + e1)
                    xr = proj_scr[rows, (1 + d) * HEAD_W:(2 + d) * HEAD_W]
                    t = jnp.exp(-jnp.abs(xr))
                    big = 1.0 / (1.0 + t)
                    small = t * big
                    pos = xr >= 0.0
                    g_scr[d, rows, :] = jnp.log(lb + (1.0 - lb) * jnp.where(pos, big, small))
                    k_scr[d, rows, :] = (1.0 - lb) * jnp.where(pos, small, big)
            _row_loop(GROUP_ROWS, ROW_BLOCK, hgrn_rows, unroll=2)

        def cum_body(t, lowest):
            rows = pl.ds(pl.multiple_of(t * TILE, TILE), TILE)
            for d in range(2):
                b = _tile_cumsum(g_scr[d, rows, :], rev=(d == 1))
                b_scr[d, rows, :] = b
                lowest = jnp.minimum(lowest, b[0:1, :] if d == 1 else b[TILE - 1:TILE, :])
            return lowest
        lowest = lax.fori_loop(0, TILES, cum_body, jnp.zeros((1, HEAD_W), F32), unroll=TILES)
        tile_safe = jnp.min(lowest) >= -TILE_DECAY_LIMIT

        @pl.when(tile_safe)
        def _():
            def tile_body(t, carry):
                rows = pl.ds(pl.multiple_of(t * TILE, TILE), TILE)
                q = proj_scr[rows, 0:HEAD_W]
                v_bf = proj_scr[rows, 3 * HEAD_W:4 * HEAD_W].astype(BF16)
                o_sum = None
                for d in range(2):
                    o_intra, q_t, s_loc, g_t = _tile_phase(q, k_scr[d, rows, :], v_bf, b_scr[d, rows, :], rev=(d == 1))
                    qt_scr[d, rows, :] = q_t
                    sloc_scr[d, t] = s_loc
                    gt_scr[pl.ds(d * TILES + t, 1), :] = g_t
                    o_sum = o_intra if o_sum is None else o_sum + o_intra
                o_scr[rows, :] = o_sum
                return carry
            lax.fori_loop(0, TILES, tile_body, 0, unroll=TILES)

        @pl.when(jnp.logical_not(tile_safe))
        def _():
            def tile_body(t, carry):
                rows = pl.ds(pl.multiple_of(t * TILE, TILE), TILE)
                q = proj_scr[rows, 0:HEAD_W]
                v_bf = proj_scr[rows, 3 * HEAD_W:4 * HEAD_W].astype(BF16)
                vb_scr[rows, :] = v_bf
                o_sum = None
                for d in range(2):
                    o_intra, q_in, k_out, fc = _chunk_phase(q, k_scr[d, rows, :], v_bf, g_scr[d, rows, :], rev=(d == 1))
                    b_scr[d, rows, :] = q_in
                    kout_scr[d, rows, :] = k_out
                    fc_scr[d, pl.ds(pl.multiple_of(t * CHUNKS_PER_TILE, CHUNKS_PER_TILE), CHUNKS_PER_TILE), :] = fc
                    o_sum = o_intra if o_sum is None else o_sum + o_intra
                o_scr[rows, :] = o_sum
                return carry
            lax.fori_loop(0, TILES, tile_body, 0)

            sloc_scr[...] = jnp.zeros_like(sloc_scr)
            gt_scr[...] = jnp.ones_like(gt_scr)

            def chunk_body(c, carry):
                for d in range(2):
                    cc = c if d == 0 else CHUNKS_PER_TILE - 1 - c
                    for t in range(TILES):
                        rows = pl.ds(pl.multiple_of(t * TILE + cc * CHUNK, CHUNK), CHUNK)
                        q_in = b_scr[d, rows, :]
                        s_loc = sloc_scr[d, t]
                        o_loc = lax.dot_general(q_in.astype(BF16), s_loc.astype(BF16), NT_DIMS,
                                                preferred_element_type=F32)
                        o_scr[rows, :] += o_loc
                        u_t = lax.dot_general(vb_scr[rows, :], kout_scr[d, rows, :], TN_DIMS,
                                              preferred_element_type=F32)
                        fc = fc_scr[d, pl.ds(t * CHUNKS_PER_TILE + cc, 1), :]
                        sloc_scr[d, t] = s_loc * fc + u_t
                        gt = gt_scr[d * TILES + t:d * TILES + t + 1, :]
                        qt_scr[d, rows, :] = (q_in * gt).astype(BF16)
                        gt_scr[d * TILES + t:d * TILES + t + 1, :] = gt * fc
                return carry
            lax.fori_loop(0, CHUNKS_PER_TILE, chunk_body, 0)

        for s in range(seqs_per_group):
            for d in range(2):
                s_run = s0_ref[s, d, 0].T if has_state_in else None
                order = range(tiles_per_seq) if d == 0 else range(tiles_per_seq - 1, -1, -1)
                for tt in order:
                    t = s * tiles_per_seq + tt
                    rows = pl.ds(t * TILE, TILE)
                    if s_run is None:
                        s_run = sloc_scr[d, t]
                        continue
                    o_scr[rows, :] += lax.dot_general(qt_scr[d, rows, :], s_run.astype(BF16), NT_DIMS,
                                                      preferred_element_type=F32)
                    s_run = s_run * gt_scr[d * TILES + t:d * TILES + t + 1, :] + sloc_scr[d, t]
                if has_state_out:
                    st_scr[s, d] = s_run.T

        if has_state_out:
            @pl.when(hd < GLA_HEADS)
            def _():
                sg_ref[:, 0, :, 0] = st_scr[:, :, 0:GLA_DK, :]

            @pl.when(hd >= GLA_HEADS)
            def _():
                sh_ref[:, 0, :, 0] = st_scr[...]

        def gated(rows):
            o = _rms(o_scr[rows, :]) * hn_ref[0]
            return (o * _silu(proj_scr[rows, 4 * HEAD_W:5 * HEAD_W])).astype(BF16)

        @pl.when((hd & 1) == 0)
        def _():
            def gate_rows(rows):
                mg_scr[rows, 0:HEAD_W] = gated(rows)
            _row_loop(GROUP_ROWS, ROW_BLOCK, gate_rows, unroll=2)

        @pl.when((hd & 1) == 1)
        def _():
            def gate_rows(rows):
                mg_scr[rows, HEAD_W:2 * HEAD_W] = gated(rows)
            _row_loop(GROUP_ROWS, ROW_BLOCK, gate_rows, unroll=2)

            def out_rows(rows):
                x1_ref[rows, :] += jnp.dot(mg_scr[rows, :], wout_ref[0], preferred_element_type=F32)
            _row_loop(GROUP_ROWS, MATMUL_ROWS, out_rows)

        @pl.when(hd == N_HEADS - 1)
        def _():
            gate1 = mod_ref[0, :, 2 * D_MODEL:3 * D_MODEL]

            def res_rows(rows):
                x1_ref[rows, :] = x_ref[rows, :] + gate1 * x1_ref[rows, :]
            _row_loop(GROUP_ROWS, ROW_BLOCK, res_rows)

    return kernel


def _mixer(x, mod3, mod_row, norm1, w_heads, wup, bgl, lbl, hnorm, wout_h, s0, seqs_per_group, want_states):
    groups = x.shape[0] // GROUP_ROWS
    n_seq = groups * seqs_per_group
    has_state_in = s0 is not None
    once = pl.Buffered(1)
    x_bufs = 2
    in_specs = [
        pl.BlockSpec((GROUP_ROWS, D_MODEL), lambda g, h: (g, 0), pipeline_mode=pl.Buffered(x_bufs)),
        pl.BlockSpec((1, 1, 6 * D_MODEL), lambda g, h: (mod_row(g), 0, 0)),
        pl.BlockSpec((1, D_MODEL), lambda g, h: (0, 0)),
        pl.BlockSpec((D_MODEL, PROJ_W), lambda g, h: (0, h)),
        pl.BlockSpec((1, 2, HEAD_W, HEAD_W), lambda g, h: (h, 0, 0, 0)),
        pl.BlockSpec((1, 2, HEAD_W), lambda g, h: (h, 0, 0)),
        pl.BlockSpec((1, 4, HEAD_W), lambda g, h: (h, 0, 0)),
        pl.BlockSpec((1, 1, HEAD_W), lambda g, h: (h, 0, 0)),
        pl.BlockSpec((1, 2 * HEAD_W, D_MODEL), lambda g, h: (h // 2, 0, 0)),
    ]
    args = [x, mod3, norm1, w_heads, wup, bgl, lbl, hnorm, wout_h]
    if has_state_in:
        in_specs.append(pl.BlockSpec((seqs_per_group, 2, 1, HEAD_W, HEAD_W), lambda g, h: (g, 0, h, 0, 0)))
        args.append(s0)
    out_specs = [pl.BlockSpec((GROUP_ROWS, D_MODEL), lambda g, h: (g, 0), pipeline_mode=once)]
    out_shape = [jax.ShapeDtypeStruct(x.shape, F32)]
    if want_states:
        out_specs.append(pl.BlockSpec((seqs_per_group, 1, 2, 1, GLA_DK, HEAD_W),
                                      lambda g, h: (g, 0, 0, jnp.minimum(h, GLA_HEADS - 1), 0, 0)))
        out_specs.append(pl.BlockSpec((seqs_per_group, 1, 2, 1, HEAD_W, HEAD_W),
                                      lambda g, h: (g, 0, 0, jnp.maximum(h - GLA_HEADS, 0), 0, 0)))
        out_shape.append(jax.ShapeDtypeStruct((n_seq, 1, 2, GLA_HEADS, GLA_DK, HEAD_W), F32))
        out_shape.append(jax.ShapeDtypeStruct((n_seq, 1, 2, N_HEADS - GLA_HEADS, HEAD_W, HEAD_W), F32))
    scratch = [
        pltpu.VMEM((GROUP_ROWS, D_MODEL), BF16),
        pltpu.VMEM((GROUP_ROWS, PROJ_W), F32),
        pltpu.VMEM((2, GROUP_ROWS, HEAD_W), F32),
        pltpu.VMEM((2, GROUP_ROWS, HEAD_W), F32),
        pltpu.VMEM((2, GROUP_ROWS, HEAD_W), F32),
        pltpu.VMEM((2, GROUP_ROWS, HEAD_W), BF16),
        pltpu.VMEM((2, GROUP_ROWS, HEAD_W), BF16),
        pltpu.VMEM((GROUP_ROWS, HEAD_W), BF16),
        pltpu.VMEM((2, GROUP_ROWS // CHUNK, HEAD_W), F32),
        pltpu.VMEM((GROUP_ROWS, HEAD_W), F32),
        pltpu.VMEM((2, TILES, HEAD_W, HEAD_W), F32),
        pltpu.VMEM((2 * TILES, HEAD_W), F32),
        pltpu.VMEM((seqs_per_group, 2, HEAD_W, HEAD_W), F32),
        pltpu.VMEM((GROUP_ROWS, 2 * HEAD_W), BF16),
    ]
    state_block = (seqs_per_group, 2, HEAD_W, HEAD_W)
    blocks = [((GROUP_ROWS, D_MODEL), F32, x_bufs), ((GROUP_ROWS, D_MODEL), F32, 1),
              ((D_MODEL, PROJ_W), BF16, 2), ((2 * HEAD_W, D_MODEL), BF16, 2), ((2, HEAD_W, HEAD_W), BF16, 2),
              ((6 * D_MODEL,), F32, 2), (state_block, F32, 2 * (int(has_state_in) + 2 * int(want_states)))]
    outs = pl.pallas_call(
        _make_mixer_kernel(seqs_per_group, has_state_in, want_states),
        grid=(groups, N_HEADS),
        in_specs=in_specs,
        out_specs=out_specs,
        out_shape=out_shape,
        scratch_shapes=scratch,
        compiler_params=pltpu.CompilerParams(dimension_semantics=("arbitrary", "arbitrary"),
                                             vmem_limit_bytes=_vmem_request(blocks, scratch)),
        name="mixer_ctx" if want_states else "mixer_lat",
    )(*args)
    return outs


def _make_ffn_kernel(grid_conv, seq_len):
    width = GRID_W if grid_conv else seq_len
    stride = width + CONV_GAP
    top = stride + CONV_GAP
    dys = (-1, 0, 1) if grid_conv else (0,)
    lane_tiles = FFN_TILE // HEAD_W
    lines_per_mm = MATMUL_ROWS // width
    lines_per_block = max(ROW_BLOCK // width, 1)
    block_rows = lines_per_block * width

    def kernel(x1_ref, mod_ref, n2_ref, wg_ref, wu_ref, cwg_ref, cwu_ref, cbg_ref, cbu_ref, wd_ref, fn_ref,
               y_ref, h_scr, pg_scr, pu_scr, act_scr):
        j = pl.program_id(1)

        @pl.when(j == 0)
        def _():
            shift2 = mod_ref[0, :, 3 * D_MODEL:4 * D_MODEL]
            scale2 = mod_ref[0, :, 4 * D_MODEL:5 * D_MODEL]
            n2 = n2_ref[...]

            def norm_rows(rows):
                h = _rms(x1_ref[rows, :]) * n2
                h_scr[rows, :] = (h * (1.0 + scale2) + shift2).astype(BF16)
                y_ref[rows, :] = jnp.zeros((ROW_BLOCK, D_MODEL), F32)
            _row_loop(GROUP_ROWS, ROW_BLOCK, norm_rows, unroll=2)
            pg_scr[...] = jnp.zeros_like(pg_scr)
            pu_scr[...] = jnp.zeros_like(pu_scr)

        def up_rows(rows):
            h = h_scr[rows, :]
            line0 = rows.start // width
            for scr, w_ref in ((pg_scr, wg_ref), (pu_scr, wu_ref)):
                res = jnp.dot(h, w_ref[...], preferred_element_type=F32)
                for lt in range(lane_tiles):
                    for ln in range(lines_per_mm):
                        dst = pl.ds(_aligned(top + (line0 + ln) * stride, 8), width)
                        scr[lt, dst, :] = res[ln * width:(ln + 1) * width, lt * HEAD_W:(lt + 1) * HEAD_W]
        _row_loop(GROUP_ROWS, MATMUL_ROWS, up_rows)

        def conv(scr, cw_ref, cb_ref, lt, base):
            lanes = slice(lt * HEAD_W, (lt + 1) * HEAD_W)
            acc = cb_ref[:, lanes]
            for dy in dys:
                for dx in (-1, 0, 1):
                    tap = cw_ref[(dy + 1) * 3 + (dx + 1):(dy + 1) * 3 + (dx + 2), lanes]
                    acc = acc + scr[lt, pl.ds(base + dy * stride + dx, width), :] * tap
            return acc

        def act_rows(i, carry):
            for ln in range(lines_per_block):
                line = i * lines_per_block + ln
                base = top + line * stride
                out = pl.ds(pl.multiple_of(line * width, width), width)
                for lt in range(lane_tiles):
                    cg = conv(pg_scr, cwg_ref, cbg_ref, lt, base)
                    cu = conv(pu_scr, cwu_ref, cbu_ref, lt, base)
                    act_scr[out, lt * HEAD_W:(lt + 1) * HEAD_W] = (_silu(cg) * cu).astype(BF16)
            return carry
        lax.fori_loop(0, GROUP_ROWS // block_rows, act_rows, 0, unroll=2)

        def down_rows(rows):
            y_ref[rows, :] += jnp.dot(act_scr[rows, :], wd_ref[...], preferred_element_type=F32)
        _row_loop(GROUP_ROWS, MATMUL_ROWS, down_rows)

        @pl.when(j == FFN_STEPS - 1)
        def _():
            gate2 = mod_ref[0, :, 5 * D_MODEL:6 * D_MODEL]
            fn = fn_ref[...]

            def res_rows(rows):
                y_ref[rows, :] = _rms(x1_ref[rows, :] + gate2 * y_ref[rows, :]) * fn
            _row_loop(GROUP_ROWS, ROW_BLOCK, res_rows)

    return kernel


def _conv_ffn(x1, mod3, mod_row, norm2, w_up, conv_w, conv_b, w_down, final_norm, grid_conv, seq_len):
    groups = x1.shape[0] // GROUP_ROWS
    in_specs = [
        pl.BlockSpec((GROUP_ROWS, D_MODEL), lambda g, j: (g, 0)),
        pl.BlockSpec((1, 1, 6 * D_MODEL), lambda g, j: (mod_row(g), 0, 0)),
        pl.BlockSpec((1, D_MODEL), lambda g, j: (0, 0)),
        pl.BlockSpec((D_MODEL, FFN_TILE), lambda g, j: (0, j)),
        pl.BlockSpec((D_MODEL, FFN_TILE), lambda g, j: (0, FFN_STEPS + j)),
        pl.BlockSpec((9, FFN_TILE), lambda g, j: (0, j)),
        pl.BlockSpec((9, FFN_TILE), lambda g, j: (0, FFN_STEPS + j)),
        pl.BlockSpec((1, FFN_TILE), lambda g, j: (0, j)),
        pl.BlockSpec((1, FFN_TILE), lambda g, j: (0, FFN_STEPS + j)),
        pl.BlockSpec((FFN_TILE, D_MODEL), lambda g, j: (j, 0)),
        pl.BlockSpec((1, D_MODEL), lambda g, j: (0, 0)),
    ]
    lane_tiles = FFN_TILE // HEAD_W
    width = GRID_W if grid_conv else seq_len
    pad_rows = (GROUP_ROWS // width + 2) * (width + CONV_GAP) + 2 * CONV_GAP
    scratch = [
        pltpu.VMEM((GROUP_ROWS, D_MODEL), BF16),
        pltpu.VMEM((lane_tiles, pad_rows, HEAD_W), F32),
        pltpu.VMEM((lane_tiles, pad_rows, HEAD_W), F32),
        pltpu.VMEM((GROUP_ROWS, FFN_TILE), BF16),
    ]
    blocks = [((GROUP_ROWS, D_MODEL), F32, 2), ((GROUP_ROWS, D_MODEL), F32, 2),
              ((D_MODEL, FFN_TILE), BF16, 4), ((FFN_TILE, D_MODEL), BF16, 2), ((16, FFN_TILE), F32, 8),
              ((6 * D_MODEL,), F32, 2)]
    return pl.pallas_call(
        _make_ffn_kernel(grid_conv, seq_len),
        grid=(groups, FFN_STEPS),
        in_specs=in_specs,
        out_specs=pl.BlockSpec((GROUP_ROWS, D_MODEL), lambda g, j: (g, 0)),
        out_shape=jax.ShapeDtypeStruct(x1.shape, F32),
        scratch_shapes=scratch,
        compiler_params=pltpu.CompilerParams(dimension_semantics=("arbitrary", "arbitrary"),
                                             vmem_limit_bytes=_vmem_request(blocks, scratch)),
        name="conv_ffn_lat" if grid_conv else "conv_ffn_ctx",
    )(x1, mod3, norm2, w_up, w_up, conv_w, conv_w, conv_b, conv_b, w_down, final_norm)


def _head_layout(w_in, w_gla_up, b_gla, hgrn_lb, gla_norm, hgrn_norm, w_out):
    gla_kw = GLA_HEADS * GLA_DK
    vw = GLA_HEADS * HEAD_W
    o_qa, o_ka, o_va, o_ga = 0, gla_kw, 2 * gla_kw, 2 * gla_kw + vw
    o_lr = o_ga + vw
    o_qb = o_lr + 2 * GLA_LOWRANK
    o_fb = o_qb + vw
    o_ib = o_fb + 2 * vw
    o_gb = o_ib + vw
    per_head = lambda off, width: w_in[:, off:off + GLA_HEADS * width].reshape(D_MODEL, GLA_HEADS, 1, width)
    low_rank = jnp.pad(w_in[:, o_lr:o_lr + 2 * GLA_LOWRANK], ((0, 0), (0, HEAD_W - 2 * GLA_LOWRANK)))
    gla = jnp.concatenate([
        jnp.concatenate([per_head(o_qa, GLA_DK), per_head(o_ka, GLA_DK)], axis=3),
        jnp.broadcast_to(low_rank[:, None, None, :], (D_MODEL, GLA_HEADS, 1, HEAD_W)),
        per_head(o_va, HEAD_W), per_head(o_ga, HEAD_W),
        jnp.zeros((D_MODEL, GLA_HEADS, 1, HEAD_W), w_in.dtype)], axis=2)
    hgrn = jnp.concatenate([per_head(o_qb, HEAD_W), per_head(o_fb, HEAD_W), per_head(o_fb + vw, HEAD_W),
                            per_head(o_ib, HEAD_W), per_head(o_gb, HEAD_W)], axis=2)
    w_heads = jnp.concatenate([gla, hgrn], axis=1).reshape(D_MODEL, N_HEADS * PROJ_W).astype(BF16)

    up = w_gla_up.astype(BF16).reshape(2, GLA_LOWRANK, GLA_HEADS, GLA_DK).transpose(2, 0, 1, 3)
    wup = jnp.stack([jnp.pad(up[:, d], ((0, N_HEADS - GLA_HEADS), (d * GLA_LOWRANK, HEAD_W - (d + 1) * GLA_LOWRANK),
                                         (0, HEAD_W - GLA_DK))) for d in range(2)], axis=1)
    bgl = jnp.pad(b_gla.reshape(2, GLA_HEADS, GLA_DK).transpose(1, 0, 2),
                  ((0, N_HEADS - GLA_HEADS), (0, 0), (0, HEAD_W - GLA_DK)))
    lb = hgrn_lb.astype(F32).reshape(2, 2, N_HEADS - GLA_HEADS, HEAD_W)
    lbl = jnp.concatenate([jnp.zeros((GLA_HEADS, 4, HEAD_W), F32),
                           lb.transpose(2, 0, 1, 3).reshape(N_HEADS - GLA_HEADS, 4, HEAD_W)], axis=0)
    hnorm = jnp.concatenate([jnp.broadcast_to(gla_norm[None, None, :], (GLA_HEADS, 1, HEAD_W)),
                             jnp.broadcast_to(hgrn_norm[None, None, :], (N_HEADS - GLA_HEADS, 1, HEAD_W))], axis=0)
    wout_h = w_out.astype(BF16).reshape(N_HEADS // 2, 2 * HEAD_W, D_MODEL)
    return w_heads, wup, bgl, lbl, hnorm, wout_h


def kernel(x_prompt, x_sample, state_gla, state_hgrn, c, c_ctx, w_ada, b_ada, norm1, norm2, w_in, w_gla_up, b_gla, hgrn_lb, gla_norm, hgrn_norm, w_out, w_ffn_up, ffn_conv, b_ffn_conv, w_ffn_down, final_norm):
    assert w_ada.shape[0] == 1 and hgrn_lb.shape[0] == 2, "single layer only"
    n_ctx, ctx_len, _ = x_prompt.shape
    n_lat, lat_len, _ = x_sample.shape
    assert lat_len == GROUP_ROWS and GROUP_ROWS % ctx_len == 0 and n_ctx % (GROUP_ROWS // ctx_len) == 0
    assert n_lat + 1 <= 8
    ctx_per_group = GROUP_ROWS // ctx_len

    cvecs = jnp.concatenate([c_ctx[None, :], c, jnp.zeros((8 - 1 - n_lat, D_MODEL), F32)], axis=0)
    mod3 = _modulation(cvecs, w_ada[0], b_ada).reshape(8, 1, 6 * D_MODEL)

    w_heads, wup, bgl, lbl, hnorm, wout_h = _head_layout(
        w_in[0], w_gla_up[0], b_gla[0], hgrn_lb, gla_norm[0], hgrn_norm[0], w_out[0])
    s0 = jnp.concatenate([jnp.pad(state_gla[:, 0], ((0, 0), (0, 0), (0, 0), (0, HEAD_W - GLA_DK), (0, 0))),
                          state_hgrn[:, 0]], axis=2)
    w_up = w_ffn_up[0].astype(BF16)
    w_down = w_ffn_down[0].astype(BF16)
    conv_w = ffn_conv[0].reshape(9, 2 * FFN_HIDDEN)
    conv_b = b_ffn_conv
    ctx_row = lambda g: 0
    lat_row = lambda g: g + 1

    xp = x_prompt.reshape(n_ctx * ctx_len, D_MODEL)
    xs = x_sample.reshape(n_lat * lat_len, D_MODEL)
    xp1, new_gla, new_hgrn = _mixer(xp, mod3, ctx_row, norm1, w_heads, wup, bgl, lbl, hnorm, wout_h, None,
                                    ctx_per_group, True)
    (xs1,) = _mixer(xs, mod3, lat_row, norm1, w_heads, wup, bgl, lbl, hnorm, wout_h, s0, 1, False)
    yp = _conv_ffn(xp1, mod3, ctx_row, norm2, w_up, conv_w, conv_b, w_down, final_norm[None, :], False, ctx_len)
    ys = _conv_ffn(xs1, mod3, lat_row, norm2, w_up, conv_w, conv_b, w_down, final_norm[None, :], True, lat_len)

    return (yp.reshape(x_prompt.shape), ys.reshape(x_sample.shape), new_gla, new_hgrn)
```

```python
import jax
import jax.numpy as jnp
from jax import lax
from jax.experimental import pallas as pl
from jax.experimental.pallas import tpu as pltpu

F32 = jnp.float32
BF16 = jnp.bfloat16

D_MODEL = 1024
N_HEADS = 8
GLA_HEADS = 4
GLA_DK = 64
HEAD_W = 128
GLA_LOWRANK = 16
GLA_GATE_NORM = 16.0
FFN_HIDDEN = 2816
GRID_W = 64
EPS = 1e-6

GROUP_ROWS = 2048
TILE = 128
CHUNK = 16
CHUNKS_PER_TILE = TILE // CHUNK
TILE_DECAY_LIMIT = 150.0
TILES = GROUP_ROWS // TILE
ROW_BLOCK = 256
MATMUL_ROWS = 1024
UP_ROWS = 2048
PROJ_W = 5 * HEAD_W
FFN_TILE = 256
FFN_STEPS = FFN_HIDDEN // FFN_TILE
CONV_GAP = 8
VMEM_SPARE = 8 * 1024 * 1024
VMEM_CAP = 58 * 1024 * 1024

NT_DIMS = (((1,), (1,)), ((), ()))
TN_DIMS = (((0,), (0,)), ((), ()))


def _aligned(x, m):
    return x if isinstance(x, int) else pl.multiple_of(x, m)


def _vmem_request(blocks, scratch):
    def nbytes(shape, dtype):
        n = jnp.dtype(dtype).itemsize
        for d in shape:
            n *= d
        return n
    total = sum(nbytes(shape, dtype) * bufs for shape, dtype, bufs in blocks)
    total += sum(nbytes(sc.shape, sc.dtype) for sc in scratch)
    return min(total + VMEM_SPARE, VMEM_CAP)


def _row_loop(n_rows, block, body, unroll=1):
    def step(i, carry):
        body(pl.ds(pl.multiple_of(i * block, block), block))
        return carry
    lax.fori_loop(0, n_rows // block, step, 0, unroll=unroll)


def _rms(x):
    return x * lax.rsqrt(jnp.mean(x * x, axis=-1, keepdims=True) + EPS)


def _silu(x):
    return x / (1.0 + jnp.exp(-x))


def _mod_kernel(c_ref, w_ref, b_ref, o_ref):
    a = _silu(c_ref[...]).astype(BF16)
    o_ref[...] = jnp.dot(a, w_ref[...].astype(BF16), preferred_element_type=F32) + b_ref[...]


def _modulation(cvecs, w_ada, b_ada):
    n = w_ada.shape[1]
    tn = 1024
    return pl.pallas_call(
        _mod_kernel,
        grid=(n // tn,),
        in_specs=[pl.BlockSpec((8, D_MODEL), lambda j: (0, 0)),
                  pl.BlockSpec((D_MODEL, tn), lambda j: (0, j)),
                  pl.BlockSpec((1, tn), lambda j: (0, j))],
        out_specs=pl.BlockSpec((8, tn), lambda j: (0, j)),
        out_shape=jax.ShapeDtypeStruct((8, n), F32),
        name="modulation",
    )(cvecs, w_ada, b_ada)


def _chunk_phase(q, k, v_bf, g, rev):
    row = lax.broadcasted_iota(jnp.int32, (TILE, HEAD_W), 0)
    col = lax.broadcasted_iota(jnp.int32, (TILE, HEAD_W), 1)
    cpos = row & (CHUNK - 1)
    f = jnp.exp(g)
    if rev:
        fz = jnp.where(cpos == CHUNK - 1, 0.0, f)
        diff = row - col
        edge_row = 0
    else:
        fz = jnp.where(cpos == 0, 0.0, f)
        diff = col - row
        edge_row = CHUNK - 1
    b = g
    for s in (1, 2, 4, 8):
        if rev:
            b = b + jnp.where(cpos <= CHUNK - 1 - s, pltpu.roll(b, TILE - s, 0), 0.0)
        else:
            b = b + jnp.where(cpos >= s, pltpu.roll(b, s, 0), 0.0)
    tot_rows = [b[j * CHUNK + edge_row:j * CHUNK + edge_row + 1, :] for j in range(CHUNKS_PER_TILE)]
    b_tot = jnp.concatenate([jnp.broadcast_to(r, (CHUNK, HEAD_W)) for r in tot_rows], axis=0)
    fc = jnp.exp(jnp.concatenate(tot_rows, axis=0))
    q_in = q * jnp.exp(b)
    k_out = (k * jnp.exp(b_tot - b)).astype(BF16)
    w = k
    a = jnp.zeros((TILE, TILE), F32)
    shift = TILE - 1 if rev else 1
    for d in range(CHUNK):
        if d:
            w = pltpu.roll(w, shift, 0) * fz
        s = jnp.sum(q * w, axis=1, keepdims=True)
        a = jnp.where(diff == -d, s, a)
    o_intra = jnp.dot(a.astype(BF16), v_bf, preferred_element_type=F32)
    return o_intra, q_in, k_out, fc


def _tile_cumsum(g, rev):
    row = lax.broadcasted_iota(jnp.int32, (TILE, TILE), 0)
    col = lax.broadcasted_iota(jnp.int32, (TILE, TILE), 1)
    tri = jnp.where((col >= row) if rev else (col <= row), 1.0, 0.0).astype(BF16)
    hi = g.astype(BF16)
    rest = g - hi.astype(F32)
    mid = rest.astype(BF16)
    lo = (rest - mid.astype(F32)).astype(BF16)
    parts = jnp.dot(tri, jnp.concatenate([hi, mid, lo], axis=1), preferred_element_type=F32)
    return parts[:, 0:HEAD_W] + parts[:, HEAD_W:2 * HEAD_W] + parts[:, 2 * HEAD_W:3 * HEAD_W]


def _tile_phase(q, k, v_bf, b, rev):
    row = lax.broadcasted_iota(jnp.int32, (TILE, TILE), 0)
    col = lax.broadcasted_iota(jnp.int32, (TILE, TILE), 1)
    tot = b[0:1, :] if rev else b[TILE - 1:TILE, :]
    half = 0.5 * tot
    e_half = jnp.exp(half)
    q_h = q * jnp.exp(b - half)
    k_h = k * jnp.exp(half - b)
    a = lax.dot_general(q_h.astype(BF16), k_h.astype(BF16), NT_DIMS, preferred_element_type=F32)
    a = jnp.where((col >= row) if rev else (col <= row), a, 0.0).astype(BF16)
    o_intra = jnp.dot(a, v_bf, preferred_element_type=F32)
    q_t = (q_h * e_half).astype(BF16)
    k_out = (k_h * e_half).astype(BF16)
    s_loc = lax.dot_general(v_bf, k_out, TN_DIMS, preferred_element_type=F32)
    return o_intra, q_t, s_loc, jnp.exp(tot)


def _make_mixer_kernel(seqs_per_group, has_state_in, has_state_out):
    tiles_per_seq = TILES // seqs_per_group

    def kernel(*refs):
        it = iter(refs)
        x_ref = next(it); mod_ref = next(it); n1_ref = next(it); w_ref = next(it)
        wup_ref = next(it); bg_ref = next(it); lb_ref = next(it); hn_ref = next(it); wout_ref = next(it)
        s0_ref = next(it) if has_state_in else None
        x1_ref = next(it)
        sg_ref = next(it) if has_state_out else None
        sh_ref = next(it) if has_state_out else None
        (h_scr, proj_scr, g_scr, k_scr, b_scr, kout_scr, qt_scr, vb_scr, fc_scr, o_scr,
         sloc_scr, gt_scr, st_scr, mg_scr) = it

        hd = pl.program_id(1)

        @pl.when(hd == 0)
        def _():
            shift1 = mod_ref[0, :, 0:D_MODEL]
            scale1 = mod_ref[0, :, D_MODEL:2 * D_MODEL]
            n1 = n1_ref[...]

            def norm_rows(rows):
                h = _rms(x_ref[rows, :]) * n1
                h_scr[rows, :] = (h * (1.0 + scale1) + shift1).astype(BF16)
                x1_ref[rows, :] = jnp.zeros((ROW_BLOCK, D_MODEL), F32)
            _row_loop(GROUP_ROWS, ROW_BLOCK, norm_rows, unroll=2)

        @pl.when(hd < GLA_HEADS)
        def _():
            def proj_rows(rows):
                h = h_scr[rows, :]
                proj_scr[rows, 0:2 * HEAD_W] = jnp.dot(h, w_ref[:, 0:2 * HEAD_W], preferred_element_type=F32)
                proj_scr[rows, 3 * HEAD_W:5 * HEAD_W] = jnp.dot(h, w_ref[:, 2 * HEAD_W:4 * HEAD_W],
                                                                preferred_element_type=F32)
            _row_loop(GROUP_ROWS, MATMUL_ROWS, proj_rows)

            def gla_rows(rows):
                qk = proj_scr[rows, 0:HEAD_W]
                low = lax.broadcasted_iota(jnp.int32, (ROW_BLOCK, HEAD_W), 1) < GLA_DK
                proj_scr[rows, 0:HEAD_W] = jnp.where(low, qk * (GLA_DK ** -0.5), 0.0)
                k = jnp.where(low, pltpu.roll(qk, HEAD_W - GLA_DK, 1), 0.0)
                lr = proj_scr[rows, HEAD_W:2 * HEAD_W].astype(BF16)
                for d in range(2):
                    z = jnp.dot(lr, wup_ref[0, d], preferred_element_type=F32) + bg_ref[0, d:d + 1, :]
                    log_sig = jnp.minimum(z, 0.0) - jnp.log(1.0 + jnp.exp(-jnp.abs(z)))
                    g_scr[d, rows, :] = log_sig * (1.0 / GLA_GATE_NORM)
                    k_scr[d, rows, :] = k
            _row_loop(GROUP_ROWS, ROW_BLOCK, gla_rows, unroll=2)

        @pl.when(hd >= GLA_HEADS)
        def _():
            def proj_rows(rows):
                proj_scr[rows, :] = jnp.dot(h_scr[rows, :], w_ref[...], preferred_element_type=F32)
            _row_loop(GROUP_ROWS, MATMUL_ROWS, proj_rows)

            def hgrn_rows(rows):
                for d in range(2):
                    a0 = lb_ref[0, d:d + 1, :]
                    a1 = lb_ref[0, 2 + d:3 + d, :]
                    m = jnp.maximum(a0, a1)
                    e0 = jnp.exp(a0 - m)
                    e1 = jnp.exp(a1 - m)
                    lb = e0 / (e0 + e1)
                    xr = proj_scr[rows, (1 + d) * HEAD_W:(2 + d) * HEAD_W]
                    t = jnp.exp(-jnp.abs(xr))
                    big = 1.0 / (1.0 + t)
                    small = t * big
                    pos = xr >= 0.0
                    g_scr[d, rows, :] = jnp.log(lb + (1.0 - lb) * jnp.where(pos, big, small))
                    k_scr[d, rows, :] = (1.0 - lb) * jnp.where(pos, small, big)
            _row_loop(GROUP_ROWS, ROW_BLOCK, hgrn_rows, unroll=2)

        def cum_body(t, lowest):
            rows = pl.ds(pl.multiple_of(t * TILE, TILE), TILE)
            for d in range(2):
                b = _tile_cumsum(g_scr[d, rows, :], rev=(d == 1))
                b_scr[d, rows, :] = b
                lowest = jnp.minimum(lowest, b[0:1, :] if d == 1 else b[TILE - 1:TILE, :])
            return lowest
        lowest = lax.fori_loop(0, TILES, cum_body, jnp.zeros((1, HEAD_W), F32), unroll=TILES)
        tile_safe = jnp.min(lowest) >= -TILE_DECAY_LIMIT

        @pl.when(tile_safe)
        def _():
            def tile_body(t, carry):
                rows = pl.ds(pl.multiple_of(t * TILE, TILE), TILE)
                q = proj_scr[rows, 0:HEAD_W]
                v_bf = proj_scr[rows, 3 * HEAD_W:4 * HEAD_W].astype(BF16)
                o_sum = None
                for d in range(2):
                    o_intra, q_t, s_loc, g_t = _tile_phase(q, k_scr[d, rows, :], v_bf, b_scr[d, rows, :], rev=(d == 1))
                    qt_scr[d, rows, :] = q_t
                    sloc_scr[d, t] = s_loc
                    gt_scr[pl.ds(d * TILES + t, 1), :] = g_t
                    o_sum = o_intra if o_sum is None else o_sum + o_intra
                o_scr[rows, :] = o_sum
                return carry
            lax.fori_loop(0, TILES, tile_body, 0, unroll=TILES)

        @pl.when(jnp.logical_not(tile_safe))
        def _():
            def tile_body(t, carry):
                rows = pl.ds(pl.multiple_of(t * TILE, TILE), TILE)
                q = proj_scr[rows, 0:HEAD_W]
                v_bf = proj_scr[rows, 3 * HEAD_W:4 * HEAD_W].astype(BF16)
                vb_scr[rows, :] = v_bf
                o_sum = None
                for d in range(2):
                    o_intra, q_in, k_out, fc = _chunk_phase(q, k_scr[d, rows, :], v_bf, g_scr[d, rows, :], rev=(d == 1))
                    b_scr[d, rows, :] = q_in
                    kout_scr[d, rows, :] = k_out
                    fc_scr[d, pl.ds(pl.multiple_of(t * CHUNKS_PER_TILE, CHUNKS_PER_TILE), CHUNKS_PER_TILE), :] = fc
                    o_sum = o_intra if o_sum is None else o_sum + o_intra
                o_scr[rows, :] = o_sum
                return carry
            lax.fori_loop(0, TILES, tile_body, 0)

            sloc_scr[...] = jnp.zeros_like(sloc_scr)
            gt_scr[...] = jnp.ones_like(gt_scr)

            def chunk_body(c, carry):
                for d in range(2):
                    cc = c if d == 0 else CHUNKS_PER_TILE - 1 - c
                    for t in range(TILES):
                        rows = pl.ds(pl.multiple_of(t * TILE + cc * CHUNK, CHUNK), CHUNK)
                        q_in = b_scr[d, rows, :]
                        s_loc = sloc_scr[d, t]
                        o_loc = lax.dot_general(q_in.astype(BF16), s_loc.astype(BF16), NT_DIMS,
                                                preferred_element_type=F32)
                        o_scr[rows, :] += o_loc
                        u_t = lax.dot_general(vb_scr[rows, :], kout_scr[d, rows, :], TN_DIMS,
                                              preferred_element_type=F32)
                        fc = fc_scr[d, pl.ds(t * CHUNKS_PER_TILE + cc, 1), :]
                        sloc_scr[d, t] = s_loc * fc + u_t
                        gt = gt_scr[d * TILES + t:d * TILES + t + 1, :]
                        qt_scr[d, rows, :] = (q_in * gt).astype(BF16)
                        gt_scr[d * TILES + t:d * TILES + t + 1, :] = gt * fc
                return carry
            lax.fori_loop(0, CHUNKS_PER_TILE, chunk_body, 0)

        for s in range(seqs_per_group):
            for d in range(2):
                s_run = s0_ref[s, d, 0].T if has_state_in else None
                order = range(tiles_per_seq) if d == 0 else range(tiles_per_seq - 1, -1, -1)
                for tt in order:
                    t = s * tiles_per_seq + tt
                    rows = pl.ds(t * TILE, TILE)
                    if s_run is None:
                        s_run = sloc_scr[d, t]
                        continue
                    o_scr[rows, :] += lax.dot_general(qt_scr[d, rows, :], s_run.astype(BF16), NT_DIMS,
                                                      preferred_element_type=F32)
                    s_run = s_run * gt_scr[d * TILES + t:d * TILES + t + 1, :] + sloc_scr[d, t]
                if has_state_out:
                    st_scr[s, d] = s_run.T

        if has_state_out:
            @pl.when(hd < GLA_HEADS)
            def _():
                sg_ref[:, 0, :, 0] = st_scr[:, :, 0:GLA_DK, :]

            @pl.when(hd >= GLA_HEADS)
            def _():
                sh_ref[:, 0, :, 0] = st_scr[...]

        def gated(rows):
            o = _rms(o_scr[rows, :]) * hn_ref[0]
            return (o * _silu(proj_scr[rows, 4 * HEAD_W:5 * HEAD_W])).astype(BF16)

        @pl.when((hd & 1) == 0)
        def _():
            def gate_rows(rows):
                mg_scr[rows, 0:HEAD_W] = gated(rows)
            _row_loop(GROUP_ROWS, ROW_BLOCK, gate_rows, unroll=2)

        @pl.when((hd & 1) == 1)
        def _():
            def gate_rows(rows):
                mg_scr[rows, HEAD_W:2 * HEAD_W] = gated(rows)
            _row_loop(GROUP_ROWS, ROW_BLOCK, gate_rows, unroll=2)

            def out_rows(rows):
                x1_ref[rows, :] += jnp.dot(mg_scr[rows, :], wout_ref[0], preferred_element_type=F32)
            _row_loop(GROUP_ROWS, MATMUL_ROWS, out_rows)

        @pl.when(hd == N_HEADS - 1)
        def _():
            gate1 = mod_ref[0, :, 2 * D_MODEL:3 * D_MODEL]

            def res_rows(rows):
                x1_ref[rows, :] = x_ref[rows, :] + gate1 * x1_ref[rows, :]
            _row_loop(GROUP_ROWS, ROW_BLOCK, res_rows)

    return kernel


def _mixer(x, mod3, mod_row, norm1, w_heads, wup, bgl, lbl, hnorm, wout_h, s0, seqs_per_group, want_states):
    groups = x.shape[0] // GROUP_ROWS
    n_seq = groups * seqs_per_group
    has_state_in = s0 is not None
    once = pl.Buffered(1)
    x_bufs = 2
    in_specs = [
        pl.BlockSpec((GROUP_ROWS, D_MODEL), lambda g, h: (g, 0), pipeline_mode=pl.Buffered(x_bufs)),
        pl.BlockSpec((1, 1, 6 * D_MODEL), lambda g, h: (mod_row(g), 0, 0)),
        pl.BlockSpec((1, D_MODEL), lambda g, h: (0, 0)),
        pl.BlockSpec((D_MODEL, PROJ_W), lambda g, h: (0, h)),
        pl.BlockSpec((1, 2, HEAD_W, HEAD_W), lambda g, h: (h, 0, 0, 0)),
        pl.BlockSpec((1, 2, HEAD_W), lambda g, h: (h, 0, 0)),
        pl.BlockSpec((1, 4, HEAD_W), lambda g, h: (h, 0, 0)),
        pl.BlockSpec((1, 1, HEAD_W), lambda g, h: (h, 0, 0)),
        pl.BlockSpec((1, 2 * HEAD_W, D_MODEL), lambda g, h: (h // 2, 0, 0)),
    ]
    args = [x, mod3, norm1, w_heads, wup, bgl, lbl, hnorm, wout_h]
    if has_state_in:
        in_specs.append(pl.BlockSpec((seqs_per_group, 2, 1, HEAD_W, HEAD_W), lambda g, h: (g, 0, h, 0, 0)))
        args.append(s0)
    out_specs = [pl.BlockSpec((GROUP_ROWS, D_MODEL), lambda g, h: (g, 0), pipeline_mode=once)]
    out_shape = [jax.ShapeDtypeStruct(x.shape, F32)]
    if want_states:
        out_specs.append(pl.BlockSpec((seqs_per_group, 1, 2, 1, GLA_DK, HEAD_W),
                                      lambda g, h: (g, 0, 0, jnp.minimum(h, GLA_HEADS - 1), 0, 0)))
        out_specs.append(pl.BlockSpec((seqs_per_group, 1, 2, 1, HEAD_W, HEAD_W),
                                      lambda g, h: (g, 0, 0, jnp.maximum(h - GLA_HEADS, 0), 0, 0)))
        out_shape.append(jax.ShapeDtypeStruct((n_seq, 1, 2, GLA_HEADS, GLA_DK, HEAD_W), F32))
        out_shape.append(jax.ShapeDtypeStruct((n_seq, 1, 2, N_HEADS - GLA_HEADS, HEAD_W, HEAD_W), F32))
    scratch = [
        pltpu.VMEM((GROUP_ROWS, D_MODEL), BF16),
        pltpu.VMEM((GROUP_ROWS, PROJ_W), F32),
        pltpu.VMEM((2, GROUP_ROWS, HEAD_W), F32),
        pltpu.VMEM((2, GROUP_ROWS, HEAD_W), F32),
        pltpu.VMEM((2, GROUP_ROWS, HEAD_W), F32),
        pltpu.VMEM((2, GROUP_ROWS, HEAD_W), BF16),
        pltpu.VMEM((2, GROUP_ROWS, HEAD_W), BF16),
        pltpu.VMEM((GROUP_ROWS, HEAD_W), BF16),
        pltpu.VMEM((2, GROUP_ROWS // CHUNK, HEAD_W), F32),
        pltpu.VMEM((GROUP_ROWS, HEAD_W), F32),
        pltpu.VMEM((2, TILES, HEAD_W, HEAD_W), F32),
        pltpu.VMEM((2 * TILES, HEAD_W), F32),
        pltpu.VMEM((seqs_per_group, 2, HEAD_W, HEAD_W), F32),
        pltpu.VMEM((GROUP_ROWS, 2 * HEAD_W), BF16),
    ]
    state_block = (seqs_per_group, 2, HEAD_W, HEAD_W)
    blocks = [((GROUP_ROWS, D_MODEL), F32, x_bufs), ((GROUP_ROWS, D_MODEL), F32, 1),
              ((D_MODEL, PROJ_W), BF16, 2), ((2 * HEAD_W, D_MODEL), BF16, 2), ((2, HEAD_W, HEAD_W), BF16, 2),
              ((6 * D_MODEL,), F32, 2), (state_block, F32, 2 * (int(has_state_in) + 2 * int(want_states)))]
    outs = pl.pallas_call(
        _make_mixer_kernel(seqs_per_group, has_state_in, want_states),
        grid=(groups, N_HEADS),
        in_specs=in_specs,
        out_specs=out_specs,
        out_shape=out_shape,
        scratch_shapes=scratch,
        compiler_params=pltpu.CompilerParams(dimension_semantics=("arbitrary", "arbitrary"),
                                             vmem_limit_bytes=_vmem_request(blocks, scratch)),
        name="mixer_ctx" if want_states else "mixer_lat",
    )(*args)
    return outs


def _make_ffn_kernel(grid_conv, seq_len):
    width = GRID_W if grid_conv else seq_len
    stride = width + CONV_GAP
    top = stride + CONV_GAP
    dys = (-1, 0, 1) if grid_conv else (0,)
    lane_tiles = FFN_TILE // HEAD_W
    lines_per_mm = UP_ROWS // width
    lines_per_block = max(ROW_BLOCK // width, 1)
    block_rows = lines_per_block * width

    def kernel(x1_ref, mod_ref, n2_ref, wg_ref, wu_ref, cwg_ref, cwu_ref, cbg_ref, cbu_ref, wd_ref, fn_ref,
               y_ref, h_scr, pg_scr, pu_scr, act_scr):
        j = pl.program_id(1)

        @pl.when(j == 0)
        def _():
            shift2 = mod_ref[0, :, 3 * D_MODEL:4 * D_MODEL]
            scale2 = mod_ref[0, :, 4 * D_MODEL:5 * D_MODEL]
            n2 = n2_ref[...]

            def norm_rows(rows):
                h = _rms(x1_ref[rows, :]) * n2
                h_scr[rows, :] = (h * (1.0 + scale2) + shift2).astype(BF16)
                y_ref[rows, :] = jnp.zeros((ROW_BLOCK, D_MODEL), F32)
            _row_loop(GROUP_ROWS, ROW_BLOCK, norm_rows, unroll=2)
            pg_scr[...] = jnp.zeros_like(pg_scr)
            pu_scr[...] = jnp.zeros_like(pu_scr)

        def up_rows(rows):
            h = h_scr[rows, :]
            line0 = rows.start // width
            for scr, w_ref in ((pg_scr, wg_ref), (pu_scr, wu_ref)):
                res = jnp.dot(h, w_ref[...], preferred_element_type=F32)
                for lt in range(lane_tiles):
                    for ln in range(lines_per_mm):
                        dst = pl.ds(_aligned(top + (line0 + ln) * stride, 8), width)
                        scr[lt, dst, :] = res[ln * width:(ln + 1) * width, lt * HEAD_W:(lt + 1) * HEAD_W]
        _row_loop(GROUP_ROWS, UP_ROWS, up_rows)

        def conv(scr, cw_ref, cb_ref, lt, base):
            lanes = slice(lt * HEAD_W, (lt + 1) * HEAD_W)
            acc = cb_ref[:, lanes]
            for dy in dys:
                for dx in (-1, 0, 1):
                    tap = cw_ref[(dy + 1) * 3 + (dx + 1):(dy + 1) * 3 + (dx + 2), lanes]
                    acc = acc + scr[lt, pl.ds(base + dy * stride + dx, width), :] * tap
            return acc

        def act_rows(i, carry):
            for ln in range(lines_per_block):
                line = i * lines_per_block + ln
                base = top + line * stride
                out = pl.ds(pl.multiple_of(line * width, width), width)
                for lt in range(lane_tiles):
                    cg = conv(pg_scr, cwg_ref, cbg_ref, lt, base)
                    cu = conv(pu_scr, cwu_ref, cbu_ref, lt, base)
                    act_scr[out, lt * HEAD_W:(lt + 1) * HEAD_W] = (_silu(cg) * cu).astype(BF16)
            return carry
        lax.fori_loop(0, GROUP_ROWS // block_rows, act_rows, 0, unroll=2)

        def down_rows(rows):
            y_ref[rows, :] += jnp.dot(act_scr[rows, :], wd_ref[...], preferred_element_type=F32)
        _row_loop(GROUP_ROWS, MATMUL_ROWS, down_rows)

        @pl.when(j == FFN_STEPS - 1)
        def _():
            gate2 = mod_ref[0, :, 5 * D_MODEL:6 * D_MODEL]
            fn = fn_ref[...]

            def res_rows(rows):
                y_ref[rows, :] = _rms(x1_ref[rows, :] + gate2 * y_ref[rows, :]) * fn
            _row_loop(GROUP_ROWS, ROW_BLOCK, res_rows)

    return kernel


def _conv_ffn(x1, mod3, mod_row, norm2, w_up, conv_w, conv_b, w_down, final_norm, grid_conv, seq_len):
    groups = x1.shape[0] // GROUP_ROWS
    in_specs = [
        pl.BlockSpec((GROUP_ROWS, D_MODEL), lambda g, j: (g, 0)),
        pl.BlockSpec((1, 1, 6 * D_MODEL), lambda g, j: (mod_row(g), 0, 0)),
        pl.BlockSpec((1, D_MODEL), lambda g, j: (0, 0)),
        pl.BlockSpec((D_MODEL, FFN_TILE), lambda g, j: (0, j)),
        pl.BlockSpec((D_MODEL, FFN_TILE), lambda g, j: (0, FFN_STEPS + j)),
        pl.BlockSpec((9, FFN_TILE), lambda g, j: (0, j)),
        pl.BlockSpec((9, FFN_TILE), lambda g, j: (0, FFN_STEPS + j)),
        pl.BlockSpec((1, FFN_TILE), lambda g, j: (0, j)),
        pl.BlockSpec((1, FFN_TILE), lambda g, j: (0, FFN_STEPS + j)),
        pl.BlockSpec((FFN_TILE, D_MODEL), lambda g, j: (j, 0)),
        pl.BlockSpec((1, D_MODEL), lambda g, j: (0, 0)),
    ]
    lane_tiles = FFN_TILE // HEAD_W
    width = GRID_W if grid_conv else seq_len
    pad_rows = (GROUP_ROWS // width + 2) * (width + CONV_GAP) + 2 * CONV_GAP
    scratch = [
        pltpu.VMEM((GROUP_ROWS, D_MODEL), BF16),
        pltpu.VMEM((lane_tiles, pad_rows, HEAD_W), F32),
        pltpu.VMEM((lane_tiles, pad_rows, HEAD_W), F32),
        pltpu.VMEM((GROUP_ROWS, FFN_TILE), BF16),
    ]
    blocks = [((GROUP_ROWS, D_MODEL), F32, 2), ((GROUP_ROWS, D_MODEL), F32, 2),
              ((D_MODEL, FFN_TILE), BF16, 4), ((FFN_TILE, D_MODEL), BF16, 2), ((16, FFN_TILE), F32, 8),
              ((6 * D_MODEL,), F32, 2)]
    return pl.pallas_call(
        _make_ffn_kernel(grid_conv, seq_len),
        grid=(groups, FFN_STEPS),
        in_specs=in_specs,
        out_specs=pl.BlockSpec((GROUP_ROWS, D_MODEL), lambda g, j: (g, 0)),
        out_shape=jax.ShapeDtypeStruct(x1.shape, F32),
        scratch_shapes=scratch,
        compiler_params=pltpu.CompilerParams(dimension_semantics=("arbitrary", "arbitrary"),
                                             vmem_limit_bytes=_vmem_request(blocks, scratch)),
        name="conv_ffn_lat" if grid_conv else "conv_ffn_ctx",
    )(x1, mod3, norm2, w_up, w_up, conv_w, conv_w, conv_b, conv_b, w_down, final_norm)


def _head_layout(w_in, w_gla_up, b_gla, hgrn_lb, gla_norm, hgrn_norm, w_out):
    gla_kw = GLA_HEADS * GLA_DK
    vw = GLA_HEADS * HEAD_W
    o_qa, o_ka, o_va, o_ga = 0, gla_kw, 2 * gla_kw, 2 * gla_kw + vw
    o_lr = o_ga + vw
    o_qb = o_lr + 2 * GLA_LOWRANK
    o_fb = o_qb + vw
    o_ib = o_fb + 2 * vw
    o_gb = o_ib + vw
    per_head = lambda off, width: w_in[:, off:off + GLA_HEADS * width].reshape(D_MODEL, GLA_HEADS, 1, width)
    low_rank = jnp.pad(w_in[:, o_lr:o_lr + 2 * GLA_LOWRANK], ((0, 0), (0, HEAD_W - 2 * GLA_LOWRANK)))
    gla = jnp.concatenate([
        jnp.concatenate([per_head(o_qa, GLA_DK), per_head(o_ka, GLA_DK)], axis=3),
        jnp.broadcast_to(low_rank[:, None, None, :], (D_MODEL, GLA_HEADS, 1, HEAD_W)),
        per_head(o_va, HEAD_W), per_head(o_ga, HEAD_W),
        jnp.zeros((D_MODEL, GLA_HEADS, 1, HEAD_W), w_in.dtype)], axis=2)
    hgrn = jnp.concatenate([per_head(o_qb, HEAD_W), per_head(o_fb, HEAD_W), per_head(o_fb + vw, HEAD_W),
                            per_head(o_ib, HEAD_W), per_head(o_gb, HEAD_W)], axis=2)
    w_heads = jnp.concatenate([gla, hgrn], axis=1).reshape(D_MODEL, N_HEADS * PROJ_W).astype(BF16)

    up = w_gla_up.astype(BF16).reshape(2, GLA_LOWRANK, GLA_HEADS, GLA_DK).transpose(2, 0, 1, 3)
    wup = jnp.stack([jnp.pad(up[:, d], ((0, N_HEADS - GLA_HEADS), (d * GLA_LOWRANK, HEAD_W - (d + 1) * GLA_LOWRANK),
                                         (0, HEAD_W - GLA_DK))) for d in range(2)], axis=1)
    bgl = jnp.pad(b_gla.reshape(2, GLA_HEADS, GLA_DK).transpose(1, 0, 2),
                  ((0, N_HEADS - GLA_HEADS), (0, 0), (0, HEAD_W - GLA_DK)))
    lb = hgrn_lb.astype(F32).reshape(2, 2, N_HEADS - GLA_HEADS, HEAD_W)
    lbl = jnp.concatenate([jnp.zeros((GLA_HEADS, 4, HEAD_W), F32),
                           lb.transpose(2, 0, 1, 3).reshape(N_HEADS - GLA_HEADS, 4, HEAD_W)], axis=0)
    hnorm = jnp.concatenate([jnp.broadcast_to(gla_norm[None, None, :], (GLA_HEADS, 1, HEAD_W)),
                             jnp.broadcast_to(hgrn_norm[None, None, :], (N_HEADS - GLA_HEADS, 1, HEAD_W))], axis=0)
    wout_h = w_out.astype(BF16).reshape(N_HEADS // 2, 2 * HEAD_W, D_MODEL)
    return w_heads, wup, bgl, lbl, hnorm, wout_h


def kernel(x_prompt, x_sample, state_gla, state_hgrn, c, c_ctx, w_ada, b_ada, norm1, norm2, w_in, w_gla_up, b_gla, hgrn_lb, gla_norm, hgrn_norm, w_out, w_ffn_up, ffn_conv, b_ffn_conv, w_ffn_down, final_norm):
    assert w_ada.shape[0] == 1 and hgrn_lb.shape[0] == 2, "single layer only"
    n_ctx, ctx_len, _ = x_prompt.shape
    n_lat, lat_len, _ = x_sample.shape
    assert lat_len == GROUP_ROWS and GROUP_ROWS % ctx_len == 0 and n_ctx % (GROUP_ROWS // ctx_len) == 0
    assert n_lat + 1 <= 8
    ctx_per_group = GROUP_ROWS // ctx_len

    cvecs = jnp.concatenate([c_ctx[None, :], c, jnp.zeros((8 - 1 - n_lat, D_MODEL), F32)], axis=0)
    mod3 = _modulation(cvecs, w_ada[0], b_ada).reshape(8, 1, 6 * D_MODEL)

    w_heads, wup, bgl, lbl, hnorm, wout_h = _head_layout(
        w_in[0], w_gla_up[0], b_gla[0], hgrn_lb, gla_norm[0], hgrn_norm[0], w_out[0])
    s0 = jnp.concatenate([jnp.pad(state_gla[:, 0], ((0, 0), (0, 0), (0, 0), (0, HEAD_W - GLA_DK), (0, 0))),
                          state_hgrn[:, 0]], axis=2)
    w_up = w_ffn_up[0].astype(BF16)
    w_down = w_ffn_down[0].astype(BF16)
    conv_w = ffn_conv[0].reshape(9, 2 * FFN_HIDDEN)
    conv_b = b_ffn_conv
    ctx_row = lambda g: 0
    lat_row = lambda g: g + 1

    xp = x_prompt.reshape(n_ctx * ctx_len, D_MODEL)
    xs = x_sample.reshape(n_lat * lat_len, D_MODEL)
    xp1, new_gla, new_hgrn = _mixer(xp, mod3, ctx_row, norm1, w_heads, wup, bgl, lbl, hnorm, wout_h, None,
                                    ctx_per_group, True)
    (xs1,) = _mixer(xs, mod3, lat_row, norm1, w_heads, wup, bgl, lbl, hnorm, wout_h, s0, 1, False)
    yp = _conv_ffn(xp1, mod3, ctx_row, norm2, w_up, conv_w, conv_b, w_down, final_norm[None, :], False, ctx_len)
    ys = _conv_ffn(xs1, mod3, lat_row, norm2, w_up, conv_w, conv_b, w_down, final_norm[None, :], True, lat_len)

    return (yp.reshape(x_prompt.shape), ys.reshape(x_sample.shape), new_gla, new_hgrn)
```
